```python
import jax, jax.numpy as jnp
from jax import lax
import numpy as np

D_MODEL = 2048
BATCH = 1
SEQ = 16384
DEPTH = 2
DEC_BATCH = 32
DEC_SEQ = 32
PAST_LEN = 2048

CHUNK = 64
Q_BLOCK = 128
HG_HEADS = 16
HG_DK = 128
HG_DV = 128
FOX_HEADS = 16
FOX_DH = 128
D_FF = 5632
N_MOD = 9
EPS = 1e-6
MASK_VALUE = -1e30
SPLIT_SIZES = (HG_HEADS * HG_DK, HG_HEADS * HG_DK, HG_HEADS * HG_DV, HG_HEADS * HG_DV,
               FOX_HEADS * FOX_DH, FOX_HEADS * FOX_DH, FOX_HEADS * FOX_DH, FOX_HEADS,
               D_MODEL, D_MODEL)
N_PROJ = sum(SPLIT_SIZES)

kernel_name = 'hybrid_hgrn2_fox_streaming_step'


def rmsnorm(x, g):
    xf = x.astype(jnp.float32)
    y = xf * lax.rsqrt(jnp.mean(xf * xf, axis=-1, keepdims=True) + EPS) * g.astype(jnp.float32)
    return y.astype(x.dtype)


def modulate(n, shift, scale):
    return n * (1.0 + scale) + shift


def swiglu(x, w_gate, w_up, w_down):
    return (jax.nn.silu(x @ w_gate) * (x @ w_up)) @ w_down


def hgrn2_recurrence(q, log_f, k, v, s0):
    b, L, h, dk = q.shape
    dv = v.shape[-1]
    C = CHUNK if L % CHUNK == 0 else L
    n = L // C

    def to_chunks(t):
        return t.astype(jnp.float32).reshape(b, n, C, h, t.shape[-1]).swapaxes(0, 1)

    tri = jnp.tril(jnp.ones((C, C), dtype=bool))[None, :, :, None, None]

    def step(S, inp):
        qc, gc, kc, vc = inp
        cum = jnp.cumsum(gc, axis=1)
        diff = cum[:, :, None] - cum[:, None, :]
        decay = jnp.exp(jnp.where(tri, diff, MASK_VALUE))
        att = jnp.einsum('bthk,bshk,btshk->bhts', qc, kc, decay)
        intra = jnp.einsum('bhts,bshv->bthv', att, vc)
        inter = jnp.einsum('bthk,bhkv->bthv', qc * jnp.exp(cum), S)
        last = cum[:, -1]
        k_dec = kc * jnp.exp(last[:, None] - cum)
        S_new = jnp.exp(last)[..., None] * S + jnp.einsum('bshk,bshv->bhkv', k_dec, vc)
        return S_new, intra + inter

    s_fin, out = lax.scan(step, s0.astype(jnp.float32),
                          (to_chunks(q), to_chunks(log_f), to_chunks(k), to_chunks(v)))
    return out.swapaxes(0, 1).reshape(b, L, h, dv), s_fin


def fox_block(q, k, v, cq, ck, qpos, kpos):
    s = jnp.einsum('bqhd,bkhd->bhqk', q, k).astype(jnp.float32) * (FOX_DH ** -0.5)
    s = s + cq.transpose(0, 2, 1)[..., None] - ck.transpose(0, 2, 1)[:, :, None, :]
    s = jnp.where(qpos[:, None] >= kpos[None, :], s, MASK_VALUE)
    p = jax.nn.softmax(s, axis=-1)
    return jnp.einsum('bhqk,bkhd->bqhd', p.astype(v.dtype), v)


def fox_attention(q, k, v, cq, ck, qpos, kpos):
    b, L, h, d = q.shape
    if L <= Q_BLOCK or L % Q_BLOCK != 0:
        return fox_block(q, k, v, cq, ck, qpos, kpos)
    nb = L // Q_BLOCK
    qb = q.reshape(b, nb, Q_BLOCK, h, d).swapaxes(0, 1)
    cqb = cq.reshape(b, nb, Q_BLOCK, h).swapaxes(0, 1)
    pb = qpos.reshape(nb, Q_BLOCK)
    out = lax.map(lambda a: fox_block(a[0], k, v, a[1], ck, a[2], kpos), (qb, cqb, pb))
    return out.swapaxes(0, 1).reshape(b, L, h, d)


def token_mixer(n, lw, s0, past):
    b, L, _ = n.shape
    points = [sum(SPLIT_SIZES[:i + 1]) for i in range(len(SPLIT_SIZES) - 1)]
    aq, af, ai, ag, fq, fk, fv, ffl, ga, gb = jnp.split(n @ lw['w_in'], points, axis=-1)
    lb = lw['lb'].reshape(HG_HEADS, HG_DK)
    z = af.astype(jnp.float32).reshape(b, L, HG_HEADS, HG_DK)
    log_f = jnp.log(lb + (1.0 - lb) * jax.nn.sigmoid(z))
    k_in = (1.0 - lb) * jax.nn.sigmoid(-z)
    o, s_new = hgrn2_recurrence(aq.reshape(b, L, HG_HEADS, HG_DK), log_f, k_in,
                                ai.reshape(b, L, HG_HEADS, HG_DV), s0)
    o = o * lax.rsqrt(jnp.mean(o * o, axis=-1, keepdims=True) + EPS) \
        * lw['hg_norm'].astype(jnp.float32).reshape(HG_HEADS, HG_DV)
    o = (o * jax.nn.silu(ag.astype(jnp.float32).reshape(b, L, HG_HEADS, HG_DV)))
    o = o.reshape(b, L, HG_HEADS * HG_DV).astype(n.dtype)
    q = fq.reshape(b, L, FOX_HEADS, FOX_DH)
    k = fk.reshape(b, L, FOX_HEADS, FOX_DH)
    v = fv.reshape(b, L, FOX_HEADS, FOX_DH)
    lf = jax.nn.log_sigmoid(ffl.astype(jnp.float32) + lw['fox_bf'].astype(jnp.float32))
    if past is None:
        k_all, v_all, lf_all = k, v, lf
    else:
        kc, vc, lfc = past
        k_all = jnp.concatenate([kc.astype(k.dtype), k], axis=1)
        v_all = jnp.concatenate([vc.astype(v.dtype), v], axis=1)
        lf_all = jnp.concatenate([lfc.astype(jnp.float32), lf], axis=1)
    p_len = k_all.shape[1] - L
    c = jnp.cumsum(lf_all, axis=1)
    att = fox_attention(q, k_all, v_all, c[:, p_len:], c,
                        p_len + jnp.arange(L), jnp.arange(p_len + L))
    yb = att.reshape(b, L, FOX_HEADS * FOX_DH)
    merged = jax.nn.sigmoid(ga) * (o @ lw['w_ba']) + jax.nn.sigmoid(gb) * (yb @ lw['w_bb'])
    return merged @ lw['w_out'], (k, v, lf, s_new)


def layer(x, c, lw, s0, past):
    mod = (jax.nn.silu(c) @ lw['ada_w'] + lw['ada_b']).reshape(c.shape[0], N_MOD, 1, D_MODEL)
    sh1, sc1, g1, sh2, sc2, g2, sh3, sc3, g3 = [mod[:, i] for i in range(N_MOD)]
    h = x + 0.5 * g1 * swiglu(modulate(rmsnorm(x, lw['ln1']), sh1, sc1),
                              lw['f1g'], lw['f1u'], lw['f1d'])
    mix, st = token_mixer(modulate(rmsnorm(h, lw['ln2']), sh2, sc2), lw, s0, past)
    h = h + g2 * mix
    h = h + 0.5 * g3 * swiglu(modulate(rmsnorm(h, lw['ln3']), sh3, sc3),
                              lw['f2g'], lw['f2u'], lw['f2d'])
    return h, st


def setup_inputs(seed: int = 0) -> dict:
    key = jax.random.key(seed)
    ks = jax.random.split(key, 32)
    f32 = jnp.float32

    def nrm(k, shape, scale=1.0):
        return jax.random.normal(k, shape, f32) * scale

    d = D_MODEL
    return {
        'x_prompt': nrm(ks[0], (BATCH, SEQ, d)),
        'x_sample': nrm(ks[1], (DEC_BATCH, DEC_SEQ, d)),
        'c_prompt': nrm(ks[2], (BATCH, d)),
        'c_sample': nrm(ks[3], (DEC_BATCH, d)),
        'cache_fox_k': nrm(ks[4], (DEPTH, DEC_BATCH, PAST_LEN, FOX_HEADS, FOX_DH)),
        'cache_fox_v': nrm(ks[5], (DEPTH, DEC_BATCH, PAST_LEN, FOX_HEADS, FOX_DH)),
        'cache_fox_logf': jax.nn.log_sigmoid(nrm(ks[6], (DEPTH, DEC_BATCH, PAST_LEN, FOX_HEADS)) + 1.0),
        'state_hgrn': nrm(ks[7], (DEPTH, DEC_BATCH, HG_HEADS, HG_DK, HG_DV), 0.5),
        'ada_w': nrm(ks[8], (DEPTH, d, N_MOD * d), 0.5 * d ** -0.5),
        'ada_b': nrm(ks[9], (DEPTH, N_MOD * d), 0.01),
        'ln_ffn1': 1.0 + nrm(ks[10], (DEPTH, d), 0.01),
        'ln_mix': 1.0 + nrm(ks[11], (DEPTH, d), 0.01),
        'ln_ffn2': 1.0 + nrm(ks[12], (DEPTH, d), 0.01),
        'ffn1_w_gate': nrm(ks[13], (DEPTH, d, D_FF), d ** -0.5),
        'ffn1_w_up': nrm(ks[14], (DEPTH, d, D_FF), d ** -0.5),
        'ffn1_w_down': nrm(ks[15], (DEPTH, D_FF, d), D_FF ** -0.5),
        'ffn2_w_gate': nrm(ks[16], (DEPTH, d, D_FF), d ** -0.5),
        'ffn2_w_up': nrm(ks[17], (DEPTH, d, D_FF), d ** -0.5),
        'ffn2_w_down': nrm(ks[18], (DEPTH, D_FF, d), D_FF ** -0.5),
        'w_in': nrm(ks[19], (DEPTH, d, N_PROJ), d ** -0.5),
        'hgrn_lb_logits': nrm(ks[20], (DEPTH, HG_HEADS * HG_DK), 0.1),
        'hgrn_norm_g': 1.0 + nrm(ks[21], (DEPTH, HG_HEADS * HG_DV), 0.01),
        'fox_f_bias': 1.0 + nrm(ks[22], (DEPTH, FOX_HEADS), 0.1),
        'w_branch_a': nrm(ks[23], (DEPTH, HG_HEADS * HG_DV, d), (HG_HEADS * HG_DV) ** -0.5),
        'w_branch_b': nrm(ks[24], (DEPTH, FOX_HEADS * FOX_DH, d), (FOX_HEADS * FOX_DH) ** -0.5),
        'w_out': nrm(ks[25], (DEPTH, d, d), d ** -0.5),
        'ln_final': 1.0 + nrm(ks[26], (d,), 0.01),
    }


def reference(x_prompt, x_sample, c_prompt, c_sample, cache_fox_k, cache_fox_v, cache_fox_logf,
              state_hgrn, ada_w, ada_b, ln_ffn1, ln_mix, ln_ffn2, ffn1_w_gate, ffn1_w_up,
              ffn1_w_down, ffn2_w_gate, ffn2_w_up, ffn2_w_down, w_in, hgrn_lb_logits,
              hgrn_norm_g, fox_f_bias, w_branch_a, w_branch_b, w_out, ln_final):
    lb_soft = jax.nn.softmax(hgrn_lb_logits.astype(jnp.float32), axis=0)
    lower_bounds = jnp.cumsum(lb_soft, axis=0) - lb_soft[0:1]
    hp, hs = x_prompt, x_sample
    kp_l, vp_l, lfp_l, sp_l, ks_l, vs_l, lfs_l, ss_l = [], [], [], [], [], [], [], []
    for l in range(DEPTH):
        lw = {
            'ada_w': ada_w[l], 'ada_b': ada_b[l],
            'ln1': ln_ffn1[l], 'ln2': ln_mix[l], 'ln3': ln_ffn2[l],
            'f1g': ffn1_w_gate[l], 'f1u': ffn1_w_up[l], 'f1d': ffn1_w_down[l],
            'f2g': ffn2_w_gate[l], 'f2u': ffn2_w_up[l], 'f2d': ffn2_w_down[l],
            'w_in': w_in[l], 'lb': lower_bounds[l], 'hg_norm': hgrn_norm_g[l],
            'fox_bf': fox_f_bias[l], 'w_ba': w_branch_a[l], 'w_bb': w_branch_b[l],
            'w_out': w_out[l],
        }
        s0 = jnp.zeros((x_prompt.shape[0], HG_HEADS, HG_DK, HG_DV), jnp.float32)
        hp, (kp, vp, lfp, sp) = layer(hp, c_prompt, lw, s0, None)
        hs, (kss, vss, lfs, sss) = layer(hs, c_sample, lw, state_hgrn[l],
                                         (cache_fox_k[l], cache_fox_v[l], cache_fox_logf[l]))
        kp_l.append(kp); vp_l.append(vp); lfp_l.append(lfp); sp_l.append(sp)
        ks_l.append(kss); vs_l.append(vss); lfs_l.append(lfs); ss_l.append(sss)
    y_prompt = rmsnorm(hp, ln_final)
    y_sample = rmsnorm(hs, ln_final)
    return (y_prompt, y_sample,
            jnp.stack(kp_l), jnp.stack(vp_l), jnp.stack(lfp_l), jnp.stack(sp_l),
            jnp.stack(ks_l), jnp.stack(vs_l), jnp.stack(lfs_l), jnp.stack(ss_l))
```

```python
import functools

import jax
import jax.numpy as jnp
from jax import lax
from jax.experimental import pallas as pl
from jax.experimental.pallas import tpu as pltpu

EPS = 1e-6
MASK_VALUE = -1e30
N_MOD = 9
LANES = 128
SUBLANES = 8
V7X_VMEM_BYTES = 64 * 1024 * 1024
VMEM_LIMIT = V7X_VMEM_BYTES * 3 // 4
HGRN_SUB = 16

F32 = jnp.float32
BF16 = jnp.bfloat16


def _params(*sem):
    return pltpu.CompilerParams(dimension_semantics=sem, vmem_limit_bytes=VMEM_LIMIT)


def _sigmoid(x):
    return 1.0 / (1.0 + jnp.exp(-x))


def _dot(a, b):
    return jnp.dot(a, b, preferred_element_type=F32)


def _dot_nt(a, b):
    return lax.dot_general(a, b, (((1,), (1,)), ((), ())), preferred_element_type=F32)


def _dot_tn(a, b):
    return lax.dot_general(a, b, (((0,), (0,)), ((), ())), preferred_element_type=F32)


def _split3(x):
    x1 = x.astype(BF16)
    r1 = x - x1.astype(F32)
    x2 = r1.astype(BF16)
    x3 = (r1 - x2.astype(F32)).astype(BF16)
    return x1, x2, x3


def _pick(is_prompt, p_ref, s_ref):
    return jnp.where(is_prompt, p_ref[...][None], s_ref[...])


def _mod_specs(k, npt, gt, n, col=None):
    if col is None:
        p = pl.BlockSpec((None, None, 1, n), lambda i, *_: (0, k, 0, 0))
        s = pl.BlockSpec((gt, None, 1, n), lambda i, *_: (jnp.maximum(i - npt, 0), k, 0, 0))
    else:
        p = pl.BlockSpec((None, None, 1, n), lambda i, j: (0, k, 0, j))
        s = pl.BlockSpec((gt, None, 1, n), lambda i, j: (jnp.maximum(i - npt, 0), k, 0, j))
    return p, s


def _ada_kernel(c_ref, w_ref, b_ref, o_ref):
    c = c_ref[...]
    a = (c * _sigmoid(c)).astype(BF16)
    o_ref[...] = _dot(a, w_ref[...].astype(BF16)) + b_ref[...]


def _ada(c_pad, w, b):
    m, d = c_pad.shape
    n = w.shape[1]
    tn = 1024 if n % 1024 == 0 else n
    return pl.pallas_call(
        _ada_kernel,
        grid=(n // tn,),
        in_specs=[pl.BlockSpec((m, d), lambda j: (0, 0)),
                  pl.BlockSpec((d, tn), lambda j: (0, j)),
                  pl.BlockSpec((1, tn), lambda j: (0, j))],
        out_specs=pl.BlockSpec((m, tn), lambda j: (0, j)),
        out_shape=jax.ShapeDtypeStruct((m, n), F32),
        compiler_params=_params("arbitrary"),
        name="ada",
    )(c_pad, w, b.reshape(1, n))


def _normmod_kernel(x_ref, g_ref, shp_ref, shs_ref, scp_ref, scs_ref, o_ref, *, npt, gt):
    is_p = pl.program_id(0) < npt
    x = x_ref[...]
    tm, d = x.shape
    y = x * lax.rsqrt(jnp.mean(x * x, axis=-1, keepdims=True) + EPS) * g_ref[...]
    sh = _pick(is_p, shp_ref, shs_ref)
    sc = _pick(is_p, scp_ref, scs_ref)
    y = y.reshape(gt, tm // gt, d) * (1.0 + sc) + sh
    o_ref[...] = y.reshape(tm, d).astype(o_ref.dtype)


def _normmod(x, g, mod_p, mod_s, k_shift, k_scale, *, tm, npt, gt):
    t, d = x.shape
    shp, shs = _mod_specs(k_shift, npt, gt, d)
    scp, scs = _mod_specs(k_scale, npt, gt, d)
    return pl.pallas_call(
        functools.partial(_normmod_kernel, npt=npt, gt=gt),
        grid=(t // tm,),
        in_specs=[pl.BlockSpec((tm, d), lambda i: (i, 0)),
                  pl.BlockSpec((1, d), lambda i: (0, 0)),
                  shp, shs, scp, scs],
        out_specs=pl.BlockSpec((tm, d), lambda i: (i, 0)),
        out_shape=jax.ShapeDtypeStruct((t, d), BF16),
        compiler_params=_params("parallel"),
        name="normmod",
    )(x, g.reshape(1, d), mod_p, mod_s, mod_p, mod_s)


def _rmsnorm_kernel(x_ref, g_ref, o_ref):
    x = x_ref[...]
    o_ref[...] = x * lax.rsqrt(jnp.mean(x * x, axis=-1, keepdims=True) + EPS) * g_ref[...]


def _final_norm(x, g, *, row0, rows, tm):
    d = x.shape[1]
    b0 = row0 // tm
    return pl.pallas_call(
        _rmsnorm_kernel,
        grid=(rows // tm,),
        in_specs=[pl.BlockSpec((tm, d), lambda i: (b0 + i, 0)),
                  pl.BlockSpec((1, d), lambda i: (0, 0))],
        out_specs=pl.BlockSpec((tm, d), lambda i: (i, 0)),
        out_shape=jax.ShapeDtypeStruct((rows, d), F32),
        compiler_params=_params("parallel"),
        name="final_norm",
    )(x, g.reshape(1, d))


def _ffn_kernel(n_ref, x_ref, wg_ref, wu_ref, wd_ref, gp_ref, gs_ref, o_ref, acc_ref, *, npt, gt):
    i, j = pl.program_id(0), pl.program_id(1)

    @pl.when(j == 0)
    def _():
        acc_ref[...] = jnp.zeros_like(acc_ref)

    n = n_ref[...]
    a = _dot(n, wg_ref[...])
    b = _dot(n, wu_ref[...])
    acc_ref[...] += _dot((a * _sigmoid(a) * b).astype(BF16), wd_ref[...])

    @pl.when(j == pl.num_programs(1) - 1)
    def _():
        tm, d = acc_ref.shape
        gate = _pick(i < npt, gp_ref, gs_ref)
        y = (0.5 * gate) * acc_ref[...].reshape(gt, tm // gt, d)
        o_ref[...] = x_ref[...] + y.reshape(tm, d)


def _ffn(n, x, wg, wu, wd, mod_p, mod_s, k_gate, *, tm, npt, gt):
    t, d = x.shape
    f = wg.shape[1]
    tf = 512 if f % 512 == 0 else f
    gp, gs = _mod_specs(k_gate, npt, gt, d)
    return pl.pallas_call(
        functools.partial(_ffn_kernel, npt=npt, gt=gt),
        grid=(t // tm, f // tf),
        in_specs=[pl.BlockSpec((tm, d), lambda i, j: (i, 0)),
                  pl.BlockSpec((tm, d), lambda i, j: (i, 0)),
                  pl.BlockSpec((d, tf), lambda i, j: (0, j)),
                  pl.BlockSpec((d, tf), lambda i, j: (0, j)),
                  pl.BlockSpec((tf, d), lambda i, j: (j, 0)),
                  gp, gs],
        out_specs=pl.BlockSpec((tm, d), lambda i, j: (i, 0)),
        out_shape=jax.ShapeDtypeStruct((t, d), F32),
        scratch_shapes=[pltpu.VMEM((tm, d), F32)],
        compiler_params=_params("parallel", "arbitrary"),
        name="ffn",
    )(n, x, wg, wu, wd, mod_p, mod_s)


def _mm_kernel(x_ref, w_ref, *o_refs):
    r = _dot(x_ref[...], w_ref[...])
    for o_ref in o_refs:
        o_ref[...] = r.astype(o_ref.dtype)


def _mm(x, w, out_dtypes, *, tm, tn):
    t, d = x.shape
    n = w.shape[1]
    outs = pl.pallas_call(
        _mm_kernel,
        grid=(t // tm, n // tn),
        in_specs=[pl.BlockSpec((tm, d), lambda i, j: (i, 0)),
                  pl.BlockSpec((d, tn), lambda i, j: (0, j))],
        out_specs=[pl.BlockSpec((tm, tn), lambda i, j: (i, j)) for _ in out_dtypes],
        out_shape=[jax.ShapeDtypeStruct((t, n), dt) for dt in out_dtypes],
        compiler_params=_params("parallel", "arbitrary"),
        name="proj",
    )(x, w)
    return outs


def _mm_logsig_kernel(x_ref, w_ref, b_ref, o_ref):
    y = _dot(x_ref[...], w_ref[...]) + b_ref[...]
    o_ref[...] = jnp.minimum(y, 0.0) - jnp.log(1.0 + jnp.exp(-jnp.abs(y)))


def _mm_logsig(x, w, b, *, tm):
    t, d = x.shape
    n = w.shape[1]
    return pl.pallas_call(
        _mm_logsig_kernel,
        grid=(t // tm,),
        in_specs=[pl.BlockSpec((tm, d), lambda i: (i, 0)),
                  pl.BlockSpec((d, n), lambda i: (0, 0)),
                  pl.BlockSpec((1, n), lambda i: (0, 0))],
        out_specs=pl.BlockSpec((tm, n), lambda i: (i, 0)),
        out_shape=jax.ShapeDtypeStruct((t, n), F32),
        compiler_params=_params("parallel"),
        name="fox_logf",
    )(x, w, b)


def _merge_kernel(o_ref, y_ref, wa_ref, wb_ref, ga_ref, gb_ref, out_ref):
    a = _dot(o_ref[...], wa_ref[...])
    b = _dot(y_ref[...], wb_ref[...])
    out_ref[...] = (_sigmoid(ga_ref[...]) * a + _sigmoid(gb_ref[...]) * b).astype(out_ref.dtype)


def _merge(o, y, wa, wb, p32, col_ga, col_gb, *, tm, tn):
    t, hw = o.shape
    fw = y.shape[1]
    d = wa.shape[1]
    ja, jb = col_ga // tn, col_gb // tn
    return pl.pallas_call(
        _merge_kernel,
        grid=(t // tm, d // tn),
        in_specs=[pl.BlockSpec((tm, hw), lambda i, j: (i, 0)),
                  pl.BlockSpec((tm, fw), lambda i, j: (i, 0)),
                  pl.BlockSpec((hw, tn), lambda i, j: (0, j)),
                  pl.BlockSpec((fw, tn), lambda i, j: (0, j)),
                  pl.BlockSpec((tm, tn), lambda i, j: (i, ja + j)),
                  pl.BlockSpec((tm, tn), lambda i, j: (i, jb + j))],
        out_specs=pl.BlockSpec((tm, tn), lambda i, j: (i, j)),
        out_shape=jax.ShapeDtypeStruct((t, d), BF16),
        compiler_params=_params("parallel", "arbitrary"),
        name="merge",
    )(o, y, wa, wb, p32, p32)


def _outproj_kernel(m_ref, w_ref, x_ref, gp_ref, gs_ref, o_ref, *, npt, gt):
    r = _dot(m_ref[...], w_ref[...])
    tm, tn = r.shape
    gate = _pick(pl.program_id(0) < npt, gp_ref, gs_ref)
    y = gate * r.reshape(gt, tm // gt, tn)
    o_ref[...] = x_ref[...] + y.reshape(tm, tn)


def _outproj(m, w, x, mod_p, mod_s, k_gate, *, tm, tn, npt, gt):
    t, d = x.shape
    gp, gs = _mod_specs(k_gate, npt, gt, tn, col=True)
    return pl.pallas_call(
        functools.partial(_outproj_kernel, npt=npt, gt=gt),
        grid=(t // tm, d // tn),
        in_specs=[pl.BlockSpec((tm, d), lambda i, j: (i, 0)),
                  pl.BlockSpec((d, tn), lambda i, j: (0, j)),
                  pl.BlockSpec((tm, tn), lambda i, j: (i, j)),
                  gp, gs],
        out_specs=pl.BlockSpec((tm, tn), lambda i, j: (i, j)),
        out_shape=jax.ShapeDtypeStruct((t, d), F32),
        compiler_params=_params("parallel", "arbitrary"),
        name="outproj",
    )(m, w, x, mod_p, mod_s)


def _hgrn_kernel(q_ref, z_ref, v_ref, ag_ref, s0_ref, lbl_ref, hgn_ref, o_ref, sout_ref, st_ref,
                 *, layer, nc):
    c = pl.program_id(2)
    C = q_ref.shape[0]
    sub = min(HGRN_SUB, C)

    @pl.when(c == 0)
    def _():
        st_ref[...] = s0_ref[...].T

    lg = lbl_ref[...]
    e = jnp.exp(lg - jnp.max(lg, axis=0, keepdims=True))
    soft = e / jnp.sum(e, axis=0, keepdims=True)
    lb = jnp.zeros((1, LANES), F32)
    for i in range(1, layer + 1):
        lb = lb + soft[i:i + 1]

    z = z_ref[...]
    ez = jnp.exp(-jnp.abs(z))
    r = 1.0 / (1.0 + ez)
    sig_pos = jnp.where(z >= 0, r, ez * r)
    sig_neg = jnp.where(z >= 0, ez * r, r)
    g = jnp.log(lb + (1.0 - lb) * sig_pos)
    kin = (1.0 - lb) * sig_neg

    row = lax.broadcasted_iota(jnp.int32, (C, C), 0)
    col = lax.broadcasted_iota(jnp.int32, (C, C), 1)
    tri = (row >= col).astype(BF16)
    g1, g2, g3 = _split3(g)
    cum = _dot(tri, g1) + _dot(tri, g2) + _dot(tri, g3)

    q = q_ref[...].astype(F32)
    v16 = v_ref[...]
    v = v16.astype(F32)
    st = st_ref[...]
    o_inter = _dot_nt((q * jnp.exp(cum)).astype(BF16), st.astype(BF16))

    sub_row = lax.broadcasted_iota(jnp.int32, (sub, 1), 0)
    outs = []
    for i in range(C // sub):
        r0 = i * sub
        cum_i, q_i, k_i, v_i = cum[r0:r0 + sub], q[r0:r0 + sub], kin[r0:r0 + sub], v[r0:r0 + sub]
        acc = o_inter[r0:r0 + sub]
        if i > 0:
            b_i = cum[r0 - 1:r0]
            qe = (q_i * jnp.exp(cum_i - b_i)).astype(BF16)
            ke = (kin[:r0] * jnp.exp(b_i - cum[:r0])).astype(BF16)
            att = _dot_nt(qe, ke)
            acc = acc + _dot(att.astype(BF16), v16[:r0])
        for j in range(sub):
            d = jnp.where(sub_row >= j, cum_i - cum_i[j:j + 1], MASK_VALUE)
            a = jnp.sum(q_i * k_i[j:j + 1] * jnp.exp(d), axis=-1, keepdims=True)
            acc = acc + a * v_i[j:j + 1]
        outs.append(acc)
    o = jnp.concatenate(outs, axis=0)

    o = o * lax.rsqrt(jnp.mean(o * o, axis=-1, keepdims=True) + EPS) * hgn_ref[...]
    ag = ag_ref[...].astype(F32)
    o_ref[...] = (o * (ag * _sigmoid(ag))).astype(o_ref.dtype)

    last = cum[C - 1:C]
    kdec = (kin * jnp.exp(last - cum)).astype(BF16)
    st_new = st * jnp.exp(last) + _dot_tn(v16, kdec)
    st_ref[...] = st_new

    @pl.when(c == nc - 1)
    def _():
        sout_ref[...] = st_new.T


def _hgrn(p16, p32, s0, lb_logits, hg_norm, *, layer, row0, nb, length, chunk,
          col_q, col_v, col_z, col_ag):
    heads = s0.shape[1]
    nc = length // chunk
    rb0 = row0 // chunk
    depth = lb_logits.shape[0]

    def rows(b, h, c):
        return rb0 + b * nc + c

    return pl.pallas_call(
        functools.partial(_hgrn_kernel, layer=layer, nc=nc),
        grid=(nb, heads, nc),
        in_specs=[pl.BlockSpec((chunk, LANES), lambda b, h, c: (rows(b, h, c), col_q + h)),
                  pl.BlockSpec((chunk, LANES), lambda b, h, c: (rows(b, h, c), col_z + h)),
                  pl.BlockSpec((chunk, LANES), lambda b, h, c: (rows(b, h, c), col_v + h)),
                  pl.BlockSpec((chunk, LANES), lambda b, h, c: (rows(b, h, c), col_ag + h)),
                  pl.BlockSpec((None, None, LANES, LANES), lambda b, h, c: (b, h, 0, 0)),
                  pl.BlockSpec((depth, LANES), lambda b, h, c: (0, h)),
                  pl.BlockSpec((1, LANES), lambda b, h, c: (0, h))],
        out_specs=[pl.BlockSpec((chunk, LANES), lambda b, h, c: (b * nc + c, h)),
                   pl.BlockSpec((None, None, LANES, LANES), lambda b, h, c: (b, h, 0, 0))],
        out_shape=[jax.ShapeDtypeStruct((nb * length, heads * LANES), BF16),
                   jax.ShapeDtypeStruct((nb, heads, LANES, LANES), F32)],
        scratch_shapes=[pltpu.VMEM((LANES, LANES), F32)],
        compiler_params=_params("parallel", "parallel", "arbitrary"),
        name="hgrn",
    )(p16, p32, p16, p32, s0, lb_logits, hg_norm)


def _cumsum_kernel(x_ref, o_ref, carry_ref):
    @pl.when(pl.program_id(0) == 0)
    def _():
        carry_ref[...] = jnp.zeros_like(carry_ref)

    x = x_ref[...]
    bw = x.shape[1]
    row = lax.broadcasted_iota(jnp.int32, (bw, bw), 0)
    col = lax.broadcasted_iota(jnp.int32, (bw, bw), 1)
    upper = (row <= col).astype(BF16)
    x1, x2, x3 = _split3(x)
    c = _dot(x1, upper) + _dot(x2, upper) + _dot(x3, upper) + carry_ref[...]
    o_ref[...] = c
    carry_ref[...] = c[:, bw - 1:bw]


def _cumsum_lanes(x):
    r, n = x.shape
    bw = 256 if n % 256 == 0 else LANES
    return pl.pallas_call(
        _cumsum_kernel,
        grid=(n // bw,),
        in_specs=[pl.BlockSpec((r, bw), lambda j: (0, j))],
        out_specs=pl.BlockSpec((r, bw), lambda j: (0, j)),
        out_shape=jax.ShapeDtypeStruct((r, n), F32),
        scratch_shapes=[pltpu.VMEM((r, 1), F32)],
        compiler_params=_params("arbitrary"),
        name="cumsum",
    )(x)


def _fox_prompt_kernel(q_ref, k_ref, v_ref, cq_ref, ck_ref, o_ref, m_ref, l_ref, acc_ref, *, scale):
    qi, ki = pl.program_id(1), pl.program_id(2)

    @pl.when(ki == 0)
    def _():
        m_ref[...] = jnp.full_like(m_ref, MASK_VALUE)
        l_ref[...] = jnp.zeros_like(l_ref)
        acc_ref[...] = jnp.zeros_like(acc_ref)

    def step(masked):
        s = _dot_nt(q_ref[...], k_ref[...]) * scale
        s = s + cq_ref[...] - ck_ref[...]
        if masked:
            tq, tk = s.shape
            row = lax.broadcasted_iota(jnp.int32, (tq, tk), 0)
            col = lax.broadcasted_iota(jnp.int32, (tq, tk), 1)
            s = jnp.where(row >= col, s, MASK_VALUE)
        m_old = m_ref[...]
        m_new = jnp.maximum(m_old, jnp.max(s, axis=-1, keepdims=True))
        alpha = jnp.exp(m_old - m_new)
        p = jnp.exp(s - m_new)
        l_ref[...] = alpha * l_ref[...] + jnp.sum(p, axis=-1, keepdims=True)
        acc_ref[...] = alpha * acc_ref[...] + _dot(p.astype(BF16), v_ref[...])
        m_ref[...] = m_new

    @pl.when(ki < qi)
    def _():
        step(False)

    @pl.when(ki == qi)
    def _():
        step(True)
        o_ref[...] = (acc_ref[...] / l_ref[...]).astype(o_ref.dtype)


def _fox_prompt(p16, kv16, cq, ck, *, heads, length, col_q, col_k, col_v, tq):
    nq = length // tq
    scale = float(LANES) ** -0.5
    return pl.pallas_call(
        functools.partial(_fox_prompt_kernel, scale=scale),
        grid=(heads, nq, nq),
        in_specs=[pl.BlockSpec((tq, LANES), lambda h, qi, ki: (qi, col_q + h)),
                  pl.BlockSpec((tq, LANES), lambda h, qi, ki: (jnp.minimum(ki, qi), col_k + h)),
                  pl.BlockSpec((tq, LANES), lambda h, qi, ki: (jnp.minimum(ki, qi), col_v + h)),
                  pl.BlockSpec((None, tq, 1), lambda h, qi, ki: (h, qi, 0)),
                  pl.BlockSpec((None, 1, tq), lambda h, qi, ki: (h, 0, jnp.minimum(ki, qi)))],
        out_specs=pl.BlockSpec((tq, LANES), lambda h, qi, ki: (qi, h)),
        out_shape=jax.ShapeDtypeStruct((length, heads * LANES), BF16),
        scratch_shapes=[pltpu.VMEM((tq, 1), F32), pltpu.VMEM((tq, 1), F32),
                        pltpu.VMEM((tq, LANES), F32)],
        compiler_params=_params("parallel", "parallel", "arbitrary"),
        name="fox_prompt",
    )(p16, kv16, kv16, cq, ck)


def _fox_sample_kernel(q_ref, kn_ref, vn_ref, kc_ref, vc_ref, cq_ref, ck_ref, o_ref, *, scale, hg, past):
    ls = q_ref.shape[0]
    row = lax.broadcasted_iota(jnp.int32, (ls, ls), 0)
    col = lax.broadcasted_iota(jnp.int32, (ls, ls), 1)
    cq_all = cq_ref[...]
    for hh in range(hg):
        lanes = slice(hh * LANES, (hh + 1) * LANES)
        q = q_ref[:, lanes]
        cq = cq_all[:, hh:hh + 1]
        ck = ck_ref[hh]
        s_c = _dot_nt(q, kc_ref[:, lanes].astype(BF16)) * scale + cq - ck[:, :past]
        s_n = _dot_nt(q, kn_ref[:, lanes]) * scale + cq - ck[:, past:past + ls]
        s_n = jnp.where(row >= col, s_n, MASK_VALUE)
        m = jnp.maximum(jnp.max(s_c, axis=-1, keepdims=True), jnp.max(s_n, axis=-1, keepdims=True))
        p_c = jnp.exp(s_c - m)
        p_n = jnp.exp(s_n - m)
        l = jnp.sum(p_c, axis=-1, keepdims=True) + jnp.sum(p_n, axis=-1, keepdims=True)
        acc = _dot(p_c.astype(BF16), vc_ref[:, lanes].astype(BF16)) + _dot(p_n.astype(BF16), vn_ref[:, lanes])
        o_ref[:, lanes] = (acc / l).astype(o_ref.dtype)


def _fox_sample(p16, kv16, cache_k, cache_v, cq, ck, *, row0, nb, ls, heads, col_q, col_k, col_v, hg):
    past = cache_k.shape[1]
    rb0 = row0 // ls
    w = hg * LANES
    scale = float(LANES) ** -0.5
    return pl.pallas_call(
        functools.partial(_fox_sample_kernel, scale=scale, hg=hg, past=past),
        grid=(nb, heads // hg),
        in_specs=[pl.BlockSpec((ls, w), lambda b, g: (rb0 + b, col_q // hg + g)),
                  pl.BlockSpec((ls, w), lambda b, g: (rb0 + b, col_k // hg + g)),
                  pl.BlockSpec((ls, w), lambda b, g: (rb0 + b, col_v // hg + g)),
                  pl.BlockSpec((None, past, w), lambda b, g: (b, 0, g)),
                  pl.BlockSpec((None, past, w), lambda b, g: (b, 0, g)),
                  pl.BlockSpec((None, None, ls, hg), lambda b, g: (b, g, 0, 0)),
                  pl.BlockSpec((None, hg, 1, ck.shape[-1]), lambda b, g: (b, g, 0, 0))],
        out_specs=pl.BlockSpec((ls, w), lambda b, g: (b, g)),
        out_shape=jax.ShapeDtypeStruct((nb * ls, heads * LANES), BF16),
        compiler_params=_params("parallel", "parallel"),
        name="fox_sample",
    )(p16, kv16, kv16, cache_k, cache_v, cq, ck)


def _row_tile(np_rows, ns_rows, ls):
    for tm in (512, 256, 128, 64, 32):
        if np_rows % tm == 0 and ns_rows % tm == 0 and tm % ls == 0:
            return tm
    raise ValueError("no row tile fits the prompt / sample row counts")


def kernel(x_prompt, x_sample, c_prompt, c_sample, cache_fox_k, cache_fox_v, cache_fox_logf,
           state_hgrn, ada_w, ada_b, ln_ffn1, ln_mix, ln_ffn2, ffn1_w_gate, ffn1_w_up,
           ffn1_w_down, ffn2_w_gate, ffn2_w_up, ffn2_w_down, w_in, hgrn_lb_logits,
           hgrn_norm_g, fox_f_bias, w_branch_a, w_branch_b, w_out, ln_final):
    bp, lp, d = x_prompt.shape
    bs, ls, _ = x_sample.shape
    depth = ada_w.shape[0]
    _, _, past, fh, fdh = cache_fox_k.shape
    _, _, hh, dk, dv = state_hgrn.shape
    assert bp == 1 and dk == LANES and dv == LANES and fdh == LANES
    assert ls % SUBLANES == 0 and d % LANES == 0
    np_rows, ns_rows = bp * lp, bs * ls
    t = np_rows + ns_rows
    hw, fw = hh * LANES, fh * LANES
    tm = _row_tile(np_rows, ns_rows, ls)
    npt, gt = np_rows // tm, tm // ls
    tn = 512 if (d % 512 == 0 and hw % 512 == 0 and fw % 512 == 0) else LANES
    chunk_p = 64 if lp % 64 == 0 else lp
    tq = 512 if lp % 512 == 0 else lp
    hg = 4 if fh % 4 == 0 else 1
    tiles = dict(tm=tm, npt=npt, gt=gt)

    x = jnp.concatenate([x_prompt.reshape(np_rows, d), x_sample.reshape(ns_rows, d)], axis=0)
    c_all = jnp.concatenate([c_prompt, c_sample], axis=0)
    m_pad = -(-c_all.shape[0] // SUBLANES) * SUBLANES
    c_pad = jnp.pad(c_all, ((0, m_pad - c_all.shape[0]), (0, 0)))
    s0_prompt = jnp.zeros((bp, hh, dk, dv), F32)

    sizes = (hw, hw, hw, hw, fw, fw, fw, fh, d, d)
    offs = [0]
    for s in sizes:
        offs.append(offs[-1] + s)
    o_aq, o_af, o_ai, o_ag, o_fq, o_fk, o_fv, o_ffl, o_ga, o_gb = offs[:10]

    outs = {k: [] for k in ("kp", "vp", "lfp", "sp", "ks", "vs", "lfs", "ss")}
    h = x
    for l in range(depth):
        wi = w_in[l]
        w16 = jnp.concatenate([wi[:, o_aq:o_aq + hw], wi[:, o_ai:o_ai + hw], wi[:, o_fq:o_fq + fw]],
                              axis=1).astype(BF16)
        w32 = jnp.concatenate([wi[:, o_af:o_af + hw], wi[:, o_ag:o_ag + hw],
                               wi[:, o_ga:o_ga + d], wi[:, o_gb:o_gb + d]], axis=1).astype(BF16)
        wkv = wi[:, o_fk:o_fk + 2 * fw].astype(BF16)
        wffl = jnp.pad(wi[:, o_ffl:o_ffl + fh], ((0, 0), (0, LANES - fh))).astype(BF16)
        bffl = jnp.pad(fox_f_bias[l], (0, LANES - fh)).reshape(1, LANES)

        mod = _ada(c_pad, ada_w[l], ada_b[l])
        mod_p = mod[0:bp].reshape(bp, N_MOD, 1, d)
        mod_s = mod[bp:bp + bs].reshape(bs, N_MOD, 1, d)

        n1 = _normmod(h, ln_ffn1[l], mod_p, mod_s, 0, 1, **tiles)
        h = _ffn(n1, h, ffn1_w_gate[l].astype(BF16), ffn1_w_up[l].astype(BF16),
                 ffn1_w_down[l].astype(BF16), mod_p, mod_s, 2, **tiles)

        n2 = _normmod(h, ln_mix[l], mod_p, mod_s, 3, 4, **tiles)
        (p16,) = _mm(n2, w16, (BF16,), tm=tm, tn=tn)
        (p32,) = _mm(n2, w32, (F32,), tm=tm, tn=tn)
        kv32, kv16 = _mm(n2, wkv, (F32, BF16), tm=tm, tn=tn)
        lf_pad = _mm_logsig(n2, wffl, bffl, tm=tm)

        cb = lambda off: off // LANES
        hg_cols = dict(col_q=0, col_v=cb(hw), col_z=0, col_ag=cb(hw))
        o_p, s_p = _hgrn(p16, p32, s0_prompt, hgrn_lb_logits, hgrn_norm_g[l].reshape(1, hw),
                         layer=l, row0=0, nb=bp, length=lp, chunk=chunk_p, **hg_cols)
        o_s, s_s = _hgrn(p16, p32, state_hgrn[l], hgrn_lb_logits, hgrn_norm_g[l].reshape(1, hw),
                         layer=l, row0=np_rows, nb=bs, length=ls, chunk=ls, **hg_cols)

        lf = lf_pad[:, :fh]
        lf_p, lf_s = lf[:np_rows], lf[np_rows:]
        c_p = _cumsum_lanes(lf_p.T)
        cache_t = cache_fox_logf[l].transpose(0, 2, 1).reshape(bs * fh, past)
        new_t = lf_s.reshape(bs, ls, fh).transpose(0, 2, 1).reshape(bs * fh, ls)
        n_all = -(-(past + ls) // LANES) * LANES
        c_s = _cumsum_lanes(jnp.concatenate(
            [cache_t, new_t, jnp.zeros((bs * fh, n_all - past - ls), F32)], axis=1))
        cq_s = c_s[:, past:past + ls].reshape(bs, fh // hg, hg, ls).transpose(0, 1, 3, 2)

        fox_cols = dict(col_q=cb(2 * hw), col_k=0, col_v=cb(fw))
        y_p = _fox_prompt(p16, kv16, c_p.reshape(fh, lp, 1), c_p.reshape(fh, 1, lp),
                          heads=fh, length=lp, tq=tq, **fox_cols)
        y_s = _fox_sample(p16, kv16, cache_fox_k[l].reshape(bs, past, fw),
                          cache_fox_v[l].reshape(bs, past, fw), cq_s, c_s.reshape(bs, fh, 1, n_all),
                          row0=np_rows, nb=bs, ls=ls, heads=fh, hg=hg, **fox_cols)

        o_all = jnp.concatenate([o_p, o_s], axis=0)
        y_all = jnp.concatenate([y_p, y_s], axis=0)
        merged = _merge(o_all, y_all, w_branch_a[l].astype(BF16), w_branch_b[l].astype(BF16),
                        p32, 2 * hw, 2 * hw + d, tm=tm, tn=tn)
        h = _outproj(merged, w_out[l].astype(BF16), h, mod_p, mod_s, 5, tn=tn, **tiles)

        n3 = _normmod(h, ln_ffn2[l], mod_p, mod_s, 6, 7, **tiles)
        h = _ffn(n3, h, ffn2_w_gate[l].astype(BF16), ffn2_w_up[l].astype(BF16),
                 ffn2_w_down[l].astype(BF16), mod_p, mod_s, 8, **tiles)

        outs["kp"].append(kv32[:np_rows, :fw].reshape(bp, lp, fh, fdh))
        outs["vp"].append(kv32[:np_rows, fw:].reshape(bp, lp, fh, fdh))
        outs["lfp"].append(lf_p.reshape(bp, lp, fh))
        outs["sp"].append(s_p)
        outs["ks"].append(kv32[np_rows:, :fw].reshape(bs, ls, fh, fdh))
        outs["vs"].append(kv32[np_rows:, fw:].reshape(bs, ls, fh, fdh))
        outs["lfs"].append(lf_s.reshape(bs, ls, fh))
        outs["ss"].append(s_s)

    y_prompt = _final_norm(h, ln_final, row0=0, rows=np_rows, tm=tm).reshape(bp, lp, d)
    y_sample = _final_norm(h, ln_final, row0=np_rows, rows=ns_rows, tm=tm).reshape(bs, ls, d)
    st = {k: jnp.stack(v) for k, v in outs.items()}
    return (y_prompt, y_sample, st["kp"], st["vp"], st["lfp"], st["sp"],
            st["ks"], st["vs"], st["lfs"], st["ss"])
```

```python
import functools

import jax
import jax.numpy as jnp
from jax import lax
from jax.experimental import pallas as pl
from jax.experimental.pallas import tpu as pltpu

EPS = 1e-6
MASK_VALUE = -1e30
N_MOD = 9
LANES = 128
SUBLANES = 8
V7X_VMEM_BYTES = 64 * 1024 * 1024
VMEM_LIMIT = V7X_VMEM_BYTES * 3 // 4
HGRN_SUB = 16
LOG2E = 1.4426950408889634

F32 = jnp.float32
BF16 = jnp.bfloat16


def _params(*sem):
    return pltpu.CompilerParams(dimension_semantics=sem, vmem_limit_bytes=VMEM_LIMIT)


def _sigmoid(x):
    return 1.0 / (1.0 + jnp.exp(-x))


def _dot(a, b):
    return jnp.dot(a, b, preferred_element_type=F32)


def _dot_nt(a, b):
    return lax.dot_general(a, b, (((1,), (1,)), ((), ())), preferred_element_type=F32)


def _dot_tn(a, b):
    return lax.dot_general(a, b, (((0,), (0,)), ((), ())), preferred_element_type=F32)


def _split3(x):
    x1 = x.astype(BF16)
    r1 = x - x1.astype(F32)
    x2 = r1.astype(BF16)
    x3 = (r1 - x2.astype(F32)).astype(BF16)
    return x1, x2, x3


def _pick(is_prompt, p_ref, s_ref):
    return jnp.where(is_prompt, p_ref[...][None], s_ref[...])


def _mod_specs(k, npt, gt, n, col=None):
    if col is None:
        p = pl.BlockSpec((None, None, 1, n), lambda i, *_: (0, k, 0, 0))
        s = pl.BlockSpec((gt, None, 1, n), lambda i, *_: (jnp.maximum(i - npt, 0), k, 0, 0))
    else:
        p = pl.BlockSpec((None, None, 1, n), lambda i, j: (0, k, 0, j))
        s = pl.BlockSpec((gt, None, 1, n), lambda i, j: (jnp.maximum(i - npt, 0), k, 0, j))
    return p, s


def _ada_kernel(c_ref, w_ref, b_ref, o_ref):
    c = c_ref[...]
    a = (c * _sigmoid(c)).astype(BF16)
    o_ref[...] = _dot(a, w_ref[...].astype(BF16)) + b_ref[...]


def _ada(c_pad, w, b):
    m, d = c_pad.shape
    n = w.shape[1]
    tn = 1024 if n % 1024 == 0 else n
    return pl.pallas_call(
        _ada_kernel,
        grid=(n // tn,),
        in_specs=[pl.BlockSpec((m, d), lambda j: (0, 0)),
                  pl.BlockSpec((d, tn), lambda j: (0, j)),
                  pl.BlockSpec((1, tn), lambda j: (0, j))],
        out_specs=pl.BlockSpec((m, tn), lambda j: (0, j)),
        out_shape=jax.ShapeDtypeStruct((m, n), F32),
        compiler_params=_params("arbitrary"),
        name="ada",
    )(c_pad, w, b.reshape(1, n))


def _normmod_kernel(x_ref, g_ref, shp_ref, shs_ref, scp_ref, scs_ref, o_ref, *, npt, gt):
    is_p = pl.program_id(0) < npt
    x = x_ref[...]
    tm, d = x.shape
    y = x * lax.rsqrt(jnp.mean(x * x, axis=-1, keepdims=True) + EPS) * g_ref[...]
    sh = _pick(is_p, shp_ref, shs_ref)
    sc = _pick(is_p, scp_ref, scs_ref)
    y = y.reshape(gt, tm // gt, d) * (1.0 + sc) + sh
    o_ref[...] = y.reshape(tm, d).astype(o_ref.dtype)


def _normmod(x, g, mod_p, mod_s, k_shift, k_scale, *, tm, npt, gt):
    t, d = x.shape
    shp, shs = _mod_specs(k_shift, npt, gt, d)
    scp, scs = _mod_specs(k_scale, npt, gt, d)
    return pl.pallas_call(
        functools.partial(_normmod_kernel, npt=npt, gt=gt),
        grid=(t // tm,),
        in_specs=[pl.BlockSpec((tm, d), lambda i: (i, 0)),
                  pl.BlockSpec((1, d), lambda i: (0, 0)),
                  shp, shs, scp, scs],
        out_specs=pl.BlockSpec((tm, d), lambda i: (i, 0)),
        out_shape=jax.ShapeDtypeStruct((t, d), BF16),
        compiler_params=_params("parallel"),
        name="normmod",
    )(x, g.reshape(1, d), mod_p, mod_s, mod_p, mod_s)


def _rmsnorm_kernel(x_ref, g_ref, o_ref):
    x = x_ref[...]
    o_ref[...] = x * lax.rsqrt(jnp.mean(x * x, axis=-1, keepdims=True) + EPS) * g_ref[...]


def _final_norm(x, g, *, row0, rows, tm):
    d = x.shape[1]
    b0 = row0 // tm
    return pl.pallas_call(
        _rmsnorm_kernel,
        grid=(rows // tm,),
        in_specs=[pl.BlockSpec((tm, d), lambda i: (b0 + i, 0)),
                  pl.BlockSpec((1, d), lambda i: (0, 0))],
        out_specs=pl.BlockSpec((tm, d), lambda i: (i, 0)),
        out_shape=jax.ShapeDtypeStruct((rows, d), F32),
        compiler_params=_params("parallel"),
        name="final_norm",
    )(x, g.reshape(1, d))


def _ffn_kernel(n_ref, x_ref, wg_ref, wu_ref, wd_ref, gp_ref, gs_ref, o_ref, acc_ref, *, npt, gt):
    i, j = pl.program_id(0), pl.program_id(1)

    @pl.when(j == 0)
    def _():
        acc_ref[...] = jnp.zeros_like(acc_ref)

    n = n_ref[...]
    a = _dot(n, wg_ref[...])
    b = _dot(n, wu_ref[...])
    acc_ref[...] += _dot((a * _sigmoid(a) * b).astype(BF16), wd_ref[...])

    @pl.when(j == pl.num_programs(1) - 1)
    def _():
        tm, d = acc_ref.shape
        gate = _pick(i < npt, gp_ref, gs_ref)
        y = (0.5 * gate) * acc_ref[...].reshape(gt, tm // gt, d)
        o_ref[...] = x_ref[...] + y.reshape(tm, d)


def _ffn(n, x, wg, wu, wd, mod_p, mod_s, k_gate, *, tm, npt, gt):
    t, d = x.shape
    f = wg.shape[1]
    tf = 512 if f % 512 == 0 else f
    gp, gs = _mod_specs(k_gate, npt, gt, d)
    return pl.pallas_call(
        functools.partial(_ffn_kernel, npt=npt, gt=gt),
        grid=(t // tm, f // tf),
        in_specs=[pl.BlockSpec((tm, d), lambda i, j: (i, 0)),
                  pl.BlockSpec((tm, d), lambda i, j: (i, 0)),
                  pl.BlockSpec((d, tf), lambda i, j: (0, j)),
                  pl.BlockSpec((d, tf), lambda i, j: (0, j)),
                  pl.BlockSpec((tf, d), lambda i, j: (j, 0)),
                  gp, gs],
        out_specs=pl.BlockSpec((tm, d), lambda i, j: (i, 0)),
        out_shape=jax.ShapeDtypeStruct((t, d), F32),
        scratch_shapes=[pltpu.VMEM((tm, d), F32)],
        compiler_params=_params("parallel", "arbitrary"),
        name="ffn",
    )(n, x, wg, wu, wd, mod_p, mod_s)


def _mm_kernel(x_ref, w_ref, *o_refs):
    r = _dot(x_ref[...], w_ref[...])
    for o_ref in o_refs:
        o_ref[...] = r.astype(o_ref.dtype)


def _mm(x, w, out_dtypes, *, tm, tn):
    t, d = x.shape
    n = w.shape[1]
    outs = pl.pallas_call(
        _mm_kernel,
        grid=(t // tm, n // tn),
        in_specs=[pl.BlockSpec((tm, d), lambda i, j: (i, 0)),
                  pl.BlockSpec((d, tn), lambda i, j: (0, j))],
        out_specs=[pl.BlockSpec((tm, tn), lambda i, j: (i, j)) for _ in out_dtypes],
        out_shape=[jax.ShapeDtypeStruct((t, n), dt) for dt in out_dtypes],
        compiler_params=_params("parallel", "arbitrary"),
        name="proj",
    )(x, w)
    return outs


def _kvproj_kernel(x_ref, w_ref, *refs, npt, ncol_k, heads, has_prev):
    kv16_ref, kp_ref, vp_ref, ks_ref, vs_ref = refs[4:] if has_prev else refs
    i, j = pl.program_id(0), pl.program_id(1)
    r = _dot(x_ref[...], w_ref[...])
    tm, tn = r.shape
    kv16_ref[...] = r.astype(kv16_ref.dtype)
    is_k, is_p = j < ncol_k, i < npt
    h0 = jnp.where(is_k, j, j - ncol_k) * (tn // LANES)

    def store(dst_ref):
        for hh in range(tn // LANES):
            dst_ref[pl.ds(h0 + hh, tm, stride=heads), :] = r[:, hh * LANES:(hh + 1) * LANES]

    for dst_ref, cond in ((kp_ref, is_k & is_p), (vp_ref, (~is_k) & is_p),
                          (ks_ref, is_k & (~is_p)), (vs_ref, (~is_k) & (~is_p))):
        pl.when(cond)(functools.partial(store, dst_ref))


def _kvproj(x, w, prev, *, layer, depth, heads, np_rows, tm, tn):
    t, d = x.shape
    n = w.shape[1]
    npt, ncol_k = np_rows // tm, n // (2 * tn)
    ns_rows = t - np_rows
    blk = (None, tm * heads, LANES)
    p_spec = pl.BlockSpec(blk, lambda i, j: (layer, jnp.minimum(i, npt - 1), 0))
    s_spec = pl.BlockSpec(blk, lambda i, j: (layer, jnp.maximum(i - npt, 0), 0))
    p_shape = jax.ShapeDtypeStruct((depth, np_rows * heads, LANES), F32)
    s_shape = jax.ShapeDtypeStruct((depth, ns_rows * heads, LANES), F32)
    has_prev = prev is not None
    any_spec = pl.BlockSpec(memory_space=pl.ANY)
    return pl.pallas_call(
        functools.partial(_kvproj_kernel, npt=npt, ncol_k=ncol_k, heads=heads, has_prev=has_prev),
        grid=(t // tm, n // tn),
        in_specs=[pl.BlockSpec((tm, d), lambda i, j: (i, 0)),
                  pl.BlockSpec((d, tn), lambda i, j: (0, j))] + ([any_spec] * 4 if has_prev else []),
        out_specs=[pl.BlockSpec((tm, tn), lambda i, j: (i, j)), p_spec, p_spec, s_spec, s_spec],
        out_shape=[jax.ShapeDtypeStruct((t, n), BF16), p_shape, p_shape, s_shape, s_shape],
        input_output_aliases={2: 1, 3: 2, 4: 3, 5: 4} if has_prev else {},
        compiler_params=_params("arbitrary", "arbitrary"),
        name="kvproj",
    )(x, w, *(prev if has_prev else ()))


def _mm_logsig_kernel(x_ref, w_ref, b_ref, o_ref):
    y = _dot(x_ref[...], w_ref[...]) + b_ref[...]
    o_ref[...] = jnp.minimum(y, 0.0) - jnp.log(1.0 + jnp.exp(-jnp.abs(y)))


def _mm_logsig(x, w, b, *, tm):
    t, d = x.shape
    n = w.shape[1]
    return pl.pallas_call(
        _mm_logsig_kernel,
        grid=(t // tm,),
        in_specs=[pl.BlockSpec((tm, d), lambda i: (i, 0)),
                  pl.BlockSpec((d, n), lambda i: (0, 0)),
                  pl.BlockSpec((1, n), lambda i: (0, 0))],
        out_specs=pl.BlockSpec((tm, n), lambda i: (i, 0)),
        out_shape=jax.ShapeDtypeStruct((t, n), F32),
        compiler_params=_params("parallel"),
        name="fox_logf",
    )(x, w, b)


def _merge_kernel(o_ref, y_ref, wa_ref, wb_ref, ga_ref, gb_ref, out_ref):
    a = _dot(o_ref[...], wa_ref[...])
    b = _dot(y_ref[...], wb_ref[...])
    out_ref[...] = (_sigmoid(ga_ref[...]) * a + _sigmoid(gb_ref[...]) * b).astype(out_ref.dtype)


def _merge(o, y, wa, wb, p32, col_ga, col_gb, *, tm, tn):
    t, hw = o.shape
    fw = y.shape[1]
    d = wa.shape[1]
    ja, jb = col_ga // tn, col_gb // tn
    return pl.pallas_call(
        _merge_kernel,
        grid=(t // tm, d // tn),
        in_specs=[pl.BlockSpec((tm, hw), lambda i, j: (i, 0)),
                  pl.BlockSpec((tm, fw), lambda i, j: (i, 0)),
                  pl.BlockSpec((hw, tn), lambda i, j: (0, j)),
                  pl.BlockSpec((fw, tn), lambda i, j: (0, j)),
                  pl.BlockSpec((tm, tn), lambda i, j: (i, ja + j)),
                  pl.BlockSpec((tm, tn), lambda i, j: (i, jb + j))],
        out_specs=pl.BlockSpec((tm, tn), lambda i, j: (i, j)),
        out_shape=jax.ShapeDtypeStruct((t, d), BF16),
        compiler_params=_params("parallel", "arbitrary"),
        name="merge",
    )(o, y, wa, wb, p32, p32)


def _outproj_kernel(m_ref, w_ref, x_ref, gp_ref, gs_ref, o_ref, *, npt, gt):
    r = _dot(m_ref[...], w_ref[...])
    tm, tn = r.shape
    gate = _pick(pl.program_id(0) < npt, gp_ref, gs_ref)
    y = gate * r.reshape(gt, tm // gt, tn)
    o_ref[...] = x_ref[...] + y.reshape(tm, tn)


def _outproj(m, w, x, mod_p, mod_s, k_gate, *, tm, tn, npt, gt):
    t, d = x.shape
    gp, gs = _mod_specs(k_gate, npt, gt, tn, col=True)
    return pl.pallas_call(
        functools.partial(_outproj_kernel, npt=npt, gt=gt),
        grid=(t // tm, d // tn),
        in_specs=[pl.BlockSpec((tm, d), lambda i, j: (i, 0)),
                  pl.BlockSpec((d, tn), lambda i, j: (0, j)),
                  pl.BlockSpec((tm, tn), lambda i, j: (i, j)),
                  gp, gs],
        out_specs=pl.BlockSpec((tm, tn), lambda i, j: (i, j)),
        out_shape=jax.ShapeDtypeStruct((t, d), F32),
        compiler_params=_params("parallel", "arbitrary"),
        name="outproj",
    )(m, w, x, mod_p, mod_s)


def _hgrn_head(q16, z, v16, ag, st, lg, hgn, *, layer):
    C = z.shape[0]
    sub = min(HGRN_SUB, C)

    e = jnp.exp(lg - jnp.max(lg, axis=0, keepdims=True))
    soft = e / jnp.sum(e, axis=0, keepdims=True)
    lb = jnp.zeros((1, LANES), F32)
    for i in range(1, layer + 1):
        lb = lb + soft[i:i + 1]

    ez = jnp.exp(-jnp.abs(z))
    r = 1.0 / (1.0 + ez)
    sig_pos = jnp.where(z >= 0, r, ez * r)
    sig_neg = jnp.where(z >= 0, ez * r, r)
    g = jnp.log(lb + (1.0 - lb) * sig_pos)
    kin = (1.0 - lb) * sig_neg

    row = lax.broadcasted_iota(jnp.int32, (C, C), 0)
    col = lax.broadcasted_iota(jnp.int32, (C, C), 1)
    tri = (row >= col).astype(BF16)
    g1, g2, g3 = _split3(g)
    cum = _dot(tri, g1) + _dot(tri, g2) + _dot(tri, g3)

    q = q16.astype(F32)
    v = v16.astype(F32)
    o_inter = _dot_nt((q * jnp.exp(cum)).astype(BF16), st.astype(BF16))

    row8 = lax.broadcasted_iota(jnp.int32, (SUBLANES, 1), 0)
    outs = []
    for i in range(C // sub):
        r0 = i * sub
        acc = o_inter[r0:r0 + sub]
        if i > 0:
            b_i = cum[r0 - 1:r0]
            qe = (q[r0:r0 + sub] * jnp.exp(cum[r0:r0 + sub] - b_i)).astype(BF16)
            ke = (kin[:r0] * jnp.exp(b_i - cum[:r0])).astype(BF16)
            att = _dot_nt(qe, ke)
            acc = acc + _dot(att.astype(BF16), v16[:r0])
        for u in range(sub // SUBLANES):
            t0 = r0 + u * SUBLANES
            cum_t, q_t = cum[t0:t0 + SUBLANES], q[t0:t0 + SUBLANES]
            acc_t = acc[u * SUBLANES:(u + 1) * SUBLANES]
            for s in range(r0, t0 + SUBLANES):
                d = cum_t - cum[s:s + 1]
                if s >= t0:
                    d = jnp.where(row8 >= s - t0, d, MASK_VALUE)
                a = jnp.sum(q_t * kin[s:s + 1] * jnp.exp(d), axis=-1, keepdims=True)
                acc_t = acc_t + a * v[s:s + 1]
            outs.append(acc_t)
    o = jnp.concatenate(outs, axis=0)

    o = o * lax.rsqrt(jnp.mean(o * o, axis=-1, keepdims=True) + EPS) * hgn
    agf = ag.astype(F32)
    o = o * (agf * _sigmoid(agf))

    last = cum[C - 1:C]
    kdec = (kin * jnp.exp(last - cum)).astype(BF16)
    return o, st * jnp.exp(last) + _dot_tn(v16, kdec)


def _hgrn_kernel(q_ref, z_ref, v_ref, ag_ref, s0_ref, lbl_ref, hgn_ref, o_ref, sout_ref, st_ref,
                 *, layer, nc, hb):
    c = pl.program_id(2)

    @pl.when(c == 0)
    def _():
        for hh in range(hb):
            st_ref[hh] = s0_ref[hh].T

    for hh in range(hb):
        lanes = slice(hh * LANES, (hh + 1) * LANES)
        o, st_new = _hgrn_head(q_ref[:, lanes], z_ref[:, lanes], v_ref[:, lanes], ag_ref[:, lanes],
                               st_ref[hh], lbl_ref[:, lanes], hgn_ref[:, lanes], layer=layer)
        o_ref[:, lanes] = o.astype(o_ref.dtype)
        st_ref[hh] = st_new

    @pl.when(c == nc - 1)
    def _():
        for hh in range(hb):
            sout_ref[hh] = st_ref[hh].T


def _hgrn(p16, p32, s0, lb_logits, hg_norm, *, layer, row0, nb, length, chunk, hb,
          col_q, col_v, col_z, col_ag):
    heads = s0.shape[1]
    nc = length // chunk
    rb0 = row0 // chunk
    depth = lb_logits.shape[0]
    w = hb * LANES

    def rows(b, h, c):
        return rb0 + b * nc + c

    return pl.pallas_call(
        functools.partial(_hgrn_kernel, layer=layer, nc=nc, hb=hb),
        grid=(nb, heads // hb, nc),
        in_specs=[pl.BlockSpec((chunk, w), lambda b, h, c: (rows(b, h, c), col_q // hb + h)),
                  pl.BlockSpec((chunk, w), lambda b, h, c: (rows(b, h, c), col_z // hb + h)),
                  pl.BlockSpec((chunk, w), lambda b, h, c: (rows(b, h, c), col_v // hb + h)),
                  pl.BlockSpec((chunk, w), lambda b, h, c: (rows(b, h, c), col_ag // hb + h)),
                  pl.BlockSpec((None, hb, LANES, LANES), lambda b, h, c: (b, h, 0, 0)),
                  pl.BlockSpec((depth, w), lambda b, h, c: (0, h)),
                  pl.BlockSpec((1, w), lambda b, h, c: (0, h))],
        out_specs=[pl.BlockSpec((chunk, w), lambda b, h, c: (b * nc + c, h)),
                   pl.BlockSpec((None, hb, LANES, LANES), lambda b, h, c: (b, h, 0, 0))],
        out_shape=[jax.ShapeDtypeStruct((nb * length, heads * LANES), BF16),
                   jax.ShapeDtypeStruct((nb, heads, LANES, LANES), F32)],
        scratch_shapes=[pltpu.VMEM((hb, LANES, LANES), F32)],
        compiler_params=_params("parallel", "parallel", "arbitrary"),
        name="hgrn",
    )(p16, p32, p16, p32, s0, lb_logits, hg_norm)


def _cumsum_kernel(x_ref, o_ref, carry_ref):
    @pl.when(pl.program_id(0) == 0)
    def _():
        carry_ref[...] = jnp.zeros_like(carry_ref)

    x = x_ref[...]
    bw = x.shape[1]
    row = lax.broadcasted_iota(jnp.int32, (bw, bw), 0)
    col = lax.broadcasted_iota(jnp.int32, (bw, bw), 1)
    upper = (row <= col).astype(BF16)
    x1, x2, x3 = _split3(x)
    c = _dot(x1, upper) + _dot(x2, upper) + _dot(x3, upper) + carry_ref[...]
    o_ref[...] = c
    carry_ref[...] = c[:, bw - 1:bw]


def _cumsum_lanes(x):
    r, n = x.shape
    bw = 256 if n % 256 == 0 else LANES
    return pl.pallas_call(
        _cumsum_kernel,
        grid=(n // bw,),
        in_specs=[pl.BlockSpec((r, bw), lambda j: (0, j))],
        out_specs=pl.BlockSpec((r, bw), lambda j: (0, j)),
        out_shape=jax.ShapeDtypeStruct((r, n), F32),
        scratch_shapes=[pltpu.VMEM((r, 1), F32)],
        compiler_params=_params("arbitrary"),
        name="cumsum",
    )(x)


def _fox_tile(q, k, v, bias, m_ref, l_ref, acc_ref, slot, masked):
    s = _dot_nt(q, k) + bias
    tq, tk = s.shape
    if masked:
        row = lax.broadcasted_iota(jnp.int32, (tq, tk), 0)
        col = lax.broadcasted_iota(jnp.int32, (tq, tk), 1)
        s = jnp.where(row >= col, s, MASK_VALUE)
    m_old = m_ref[slot]
    m_new = jnp.maximum(m_old, jnp.max(s, axis=-1, keepdims=True))
    alpha = jnp.exp2(m_old - m_new)
    if tk % LANES == 0:
        ps = [jnp.exp2(s[:, c * LANES:(c + 1) * LANES] - m_new) for c in range(tk // LANES)]
        p_lanes = functools.reduce(jnp.add, ps)
        p = jnp.concatenate(ps, axis=1)
    else:
        p = jnp.exp2(s - m_new[:, :tk])
        p_lanes = jnp.concatenate([p, jnp.zeros((tq, LANES - tk), F32)], axis=1)
    l_ref[slot] = alpha * l_ref[slot] + p_lanes
    acc_ref[slot] = alpha * acc_ref[slot] + _dot(p.astype(BF16), v)
    m_ref[slot] = m_new


def _fox_init(m_ref, l_ref, acc_ref):
    m_ref[...] = jnp.full_like(m_ref, MASK_VALUE)
    l_ref[...] = jnp.zeros_like(l_ref)
    acc_ref[...] = jnp.zeros_like(acc_ref)


def _fox_out(l_ref, acc_ref, slot):
    return acc_ref[slot] / jnp.sum(l_ref[slot], axis=-1, keepdims=True)


def _fox_prompt_kernel(q_ref, k_ref, v_ref, c_ref, o_ref, m_ref, l_ref, acc_ref, *, tb):
    qi = pl.program_id(1)
    _fox_init(m_ref, l_ref, acc_ref)
    base = pl.multiple_of(qi * (2 * tb), 2 * tb)
    c_first = c_ref[:, pl.ds(base, LANES)][:, 0:1]

    def tile(half, kstart, masked):
        bias = (c_first - c_ref[:, pl.ds(kstart, tb)]) * LOG2E
        _fox_tile(q_ref[half * tb:(half + 1) * tb], k_ref[pl.ds(kstart, tb), :],
                  v_ref[pl.ds(kstart, tb), :], bias, m_ref, l_ref, acc_ref, half, masked)

    def body(ki, carry):
        kstart = pl.multiple_of(ki * tb, tb)
        tile(0, kstart, False)
        tile(1, kstart, False)
        return carry

    lax.fori_loop(0, 2 * qi, body, 0)
    tile(0, base, True)
    tile(1, base, False)
    tile(1, pl.multiple_of(base + tb, tb), True)
    for half in range(2):
        o_ref[half * tb:(half + 1) * tb] = _fox_out(l_ref, acc_ref, half).astype(o_ref.dtype)


def _fox_prompt(p16, kv16, c, *, heads, length, col_q, col_k, col_v, tb):
    return pl.pallas_call(
        functools.partial(_fox_prompt_kernel, tb=tb),
        grid=(heads, length // (2 * tb)),
        in_specs=[pl.BlockSpec((2 * tb, LANES), lambda h, qi: (qi, col_q + h)),
                  pl.BlockSpec((length, LANES), lambda h, qi: (0, col_k + h)),
                  pl.BlockSpec((length, LANES), lambda h, qi: (0, col_v + h)),
                  pl.BlockSpec((None, 1, length), lambda h, qi: (h, 0, 0))],
        out_specs=pl.BlockSpec((2 * tb, LANES), lambda h, qi: (qi, h)),
        out_shape=jax.ShapeDtypeStruct((length, heads * LANES), BF16),
        scratch_shapes=[pltpu.VMEM((2, tb, LANES), F32), pltpu.VMEM((2, tb, LANES), F32),
                        pltpu.VMEM((2, tb, LANES), F32)],
        compiler_params=_params("parallel", "arbitrary"),
        name="fox_prompt",
    )(p16, kv16, kv16, c)


def _fox_sample_kernel(q_ref, kn_ref, vn_ref, kc_ref, vc_ref, cq_ref, ckc_ref, ckn_ref, o_ref,
                       m_ref, l_ref, acc_ref, *, heads, nck):
    c = pl.program_id(1)
    tkc = kc_ref.shape[0] // heads

    @pl.when(c == 0)
    def _():
        _fox_init(m_ref, l_ref, acc_ref)

    cq_all = cq_ref[...] * LOG2E

    def head_lanes(h):
        return slice(h * LANES, (h + 1) * LANES)

    for h in range(heads):
        k = kc_ref[pl.ds(h, tkc, stride=heads), :].astype(BF16)
        v = vc_ref[pl.ds(h, tkc, stride=heads), :].astype(BF16)
        bias = cq_all[:, h:h + 1] - ckc_ref[h] * LOG2E
        _fox_tile(q_ref[:, head_lanes(h)], k, v, bias, m_ref, l_ref, acc_ref, h, False)

    @pl.when(c == nck - 1)
    def _():
        for h in range(heads):
            bias = cq_all[:, h:h + 1] - ckn_ref[h] * LOG2E
            _fox_tile(q_ref[:, head_lanes(h)], kn_ref[:, head_lanes(h)], vn_ref[:, head_lanes(h)],
                      bias, m_ref, l_ref, acc_ref, h, True)
            o_ref[:, head_lanes(h)] = _fox_out(l_ref, acc_ref, h).astype(o_ref.dtype)


def _fox_sample(p16, kv16, cache_k, cache_v, cq, ckc, ckn, *, layer, row0, nb, ls, heads,
                col_q, col_k, col_v, tkc):
    past = cache_k.shape[2] // heads
    nck = past // tkc
    rb0 = row0 // ls
    w = heads * LANES
    stat = pltpu.VMEM((heads, ls, LANES), F32)
    return pl.pallas_call(
        functools.partial(_fox_sample_kernel, heads=heads, nck=nck),
        grid=(nb, nck),
        in_specs=[pl.BlockSpec((ls, w), lambda b, c: (rb0 + b, col_q // heads)),
                  pl.BlockSpec((ls, w), lambda b, c: (rb0 + b, col_k // heads)),
                  pl.BlockSpec((ls, w), lambda b, c: (rb0 + b, col_v // heads)),
                  pl.BlockSpec((None, None, tkc * heads, LANES), lambda b, c: (layer, b, c, 0)),
                  pl.BlockSpec((None, None, tkc * heads, LANES), lambda b, c: (layer, b, c, 0)),
                  pl.BlockSpec((None, ls, heads), lambda b, c: (b, 0, 0)),
                  pl.BlockSpec((None, heads, 1, tkc), lambda b, c: (b, 0, 0, c)),
                  pl.BlockSpec((None, heads, 1, ls), lambda b, c: (b, 0, 0, 0))],
        out_specs=pl.BlockSpec((ls, w), lambda b, c: (b, 0)),
        out_shape=jax.ShapeDtypeStruct((nb * ls, w), BF16),
        scratch_shapes=[stat, stat, stat],
        compiler_params=_params("parallel", "arbitrary"),
        name="fox_sample",
    )(p16, kv16, kv16, cache_k, cache_v, cq, ckc, ckn)


def _row_tile(np_rows, ns_rows, ls):
    for tm in (512, 256, 128, 64, 32):
        if np_rows % tm == 0 and ns_rows % tm == 0 and tm % ls == 0:
            return tm
    raise ValueError("no row tile fits the prompt / sample row counts")


def kernel(x_prompt, x_sample, c_prompt, c_sample, cache_fox_k, cache_fox_v, cache_fox_logf,
           state_hgrn, ada_w, ada_b, ln_ffn1, ln_mix, ln_ffn2, ffn1_w_gate, ffn1_w_up,
           ffn1_w_down, ffn2_w_gate, ffn2_w_up, ffn2_w_down, w_in, hgrn_lb_logits,
           hgrn_norm_g, fox_f_bias, w_branch_a, w_branch_b, w_out, ln_final):
    bp, lp, d = x_prompt.shape
    bs, ls, _ = x_sample.shape
    depth = ada_w.shape[0]
    _, _, past, fh, fdh = cache_fox_k.shape
    _, _, hh, dk, dv = state_hgrn.shape
    assert bp == 1 and dk == LANES and dv == LANES and fdh == LANES
    assert ls % SUBLANES == 0 and d % LANES == 0
    np_rows, ns_rows = bp * lp, bs * ls
    t = np_rows + ns_rows
    hw, fw = hh * LANES, fh * LANES
    tm = _row_tile(np_rows, ns_rows, ls)
    npt, gt = np_rows // tm, tm // ls
    tn = 512 if (d % 512 == 0 and hw % 512 == 0 and fw % 512 == 0) else LANES
    chunk_p = 64 if lp % 64 == 0 else lp
    tb = 512 if lp % 1024 == 0 else lp // 2
    tkc = 512 if past % 512 == 0 else past
    hb = 4 if hh % 4 == 0 else (2 if hh % 2 == 0 else 1)
    tm_kv = min(tm, 256)
    tiles = dict(tm=tm, npt=npt, gt=gt)
    q_scale = LOG2E * float(fdh) ** -0.5

    x = jnp.concatenate([x_prompt.reshape(np_rows, d), x_sample.reshape(ns_rows, d)], axis=0)
    c_all = jnp.concatenate([c_prompt, c_sample], axis=0)
    m_pad = -(-c_all.shape[0] // SUBLANES) * SUBLANES
    c_pad = jnp.pad(c_all, ((0, m_pad - c_all.shape[0]), (0, 0)))
    s0_prompt = jnp.zeros((bp, hh, dk, dv), F32)

    sizes = (hw, hw, hw, hw, fw, fw, fw, fh, d, d)
    offs = [0]
    for s in sizes:
        offs.append(offs[-1] + s)
    o_aq, o_af, o_ai, o_ag, o_fq, o_fk, o_fv, o_ffl, o_ga, o_gb = offs[:10]

    outs = {k: [] for k in ("lfp", "sp", "lfs", "ss")}
    cache_k = cache_fox_k.reshape(depth, bs, past * fh, fdh)
    cache_v = cache_fox_v.reshape(depth, bs, past * fh, fdh)
    kv_out = None
    h = x
    for l in range(depth):
        wi = w_in[l]
        w16 = jnp.concatenate([wi[:, o_aq:o_aq + hw], wi[:, o_ai:o_ai + hw],
                               wi[:, o_fq:o_fq + fw] * q_scale], axis=1).astype(BF16)
        w32 = jnp.concatenate([wi[:, o_af:o_af + hw], wi[:, o_ag:o_ag + hw],
                               wi[:, o_ga:o_ga + d], wi[:, o_gb:o_gb + d]], axis=1).astype(BF16)
        wkv = wi[:, o_fk:o_fk + 2 * fw].astype(BF16)
        wffl = jnp.pad(wi[:, o_ffl:o_ffl + fh], ((0, 0), (0, LANES - fh))).astype(BF16)
        bffl = jnp.pad(fox_f_bias[l], (0, LANES - fh)).reshape(1, LANES)

        mod = _ada(c_pad, ada_w[l], ada_b[l])
        mod_p = mod[0:bp].reshape(bp, N_MOD, 1, d)
        mod_s = mod[bp:bp + bs].reshape(bs, N_MOD, 1, d)

        n1 = _normmod(h, ln_ffn1[l], mod_p, mod_s, 0, 1, **tiles)
        h = _ffn(n1, h, ffn1_w_gate[l].astype(BF16), ffn1_w_up[l].astype(BF16),
                 ffn1_w_down[l].astype(BF16), mod_p, mod_s, 2, **tiles)

        n2 = _normmod(h, ln_mix[l], mod_p, mod_s, 3, 4, **tiles)
        (p16,) = _mm(n2, w16, (BF16,), tm=tm, tn=tn)
        (p32,) = _mm(n2, w32, (F32,), tm=tm, tn=tn)
        kv16, *kv_out = _kvproj(n2, wkv, kv_out, layer=l, depth=depth, heads=fh,
                                np_rows=np_rows, tm=tm_kv, tn=tn)
        lf_pad = _mm_logsig(n2, wffl, bffl, tm=tm)

        cb = lambda off: off // LANES
        hg_cols = dict(col_q=0, col_v=cb(hw), col_z=0, col_ag=cb(hw), hb=hb)
        o_p, s_p = _hgrn(p16, p32, s0_prompt, hgrn_lb_logits, hgrn_norm_g[l].reshape(1, hw),
                         layer=l, row0=0, nb=bp, length=lp, chunk=chunk_p, **hg_cols)
        o_s, s_s = _hgrn(p16, p32, state_hgrn[l], hgrn_lb_logits, hgrn_norm_g[l].reshape(1, hw),
                         layer=l, row0=np_rows, nb=bs, length=ls, chunk=ls, **hg_cols)

        lf = lf_pad[:, :fh]
        lf_p, lf_s = lf[:np_rows], lf[np_rows:]
        c_p = _cumsum_lanes(lf_p.T)
        cache_t = cache_fox_logf[l].transpose(0, 2, 1).reshape(bs * fh, past)
        new_t = lf_s.reshape(bs, ls, fh).transpose(0, 2, 1).reshape(bs * fh, ls)
        n_all = -(-(past + ls) // LANES) * LANES
        c_s = _cumsum_lanes(jnp.concatenate(
            [cache_t, new_t, jnp.zeros((bs * fh, n_all - past - ls), F32)], axis=1))
        cq_s = c_s[:, past:past + ls].reshape(bs, fh, ls).transpose(0, 2, 1)
        ckc_s = c_s[:, :past].reshape(bs, fh, 1, past)
        ckn_s = c_s[:, past:past + ls].reshape(bs, fh, 1, ls)

        fox_cols = dict(col_q=cb(2 * hw), col_k=0, col_v=cb(fw))
        y_p = _fox_prompt(p16, kv16, c_p.reshape(fh, 1, lp), heads=fh, length=lp, tb=tb, **fox_cols)
        y_s = _fox_sample(p16, kv16, cache_k, cache_v, cq_s, ckc_s, ckn_s, layer=l,
                          row0=np_rows, nb=bs, ls=ls, heads=fh, tkc=tkc, **fox_cols)

        o_all = jnp.concatenate([o_p, o_s], axis=0)
        y_all = jnp.concatenate([y_p, y_s], axis=0)
        merged = _merge(o_all, y_all, w_branch_a[l].astype(BF16), w_branch_b[l].astype(BF16),
                        p32, 2 * hw, 2 * hw + d, tm=tm, tn=tn)
        h = _outproj(merged, w_out[l].astype(BF16), h, mod_p, mod_s, 5, tn=tn, **tiles)

        n3 = _normmod(h, ln_ffn2[l], mod_p, mod_s, 6, 7, **tiles)
        h = _ffn(n3, h, ffn2_w_gate[l].astype(BF16), ffn2_w_up[l].astype(BF16),
                 ffn2_w_down[l].astype(BF16), mod_p, mod_s, 8, **tiles)

        outs["lfp"].append(lf_p.reshape(bp, lp, fh))
        outs["sp"].append(s_p)
        outs["lfs"].append(lf_s.reshape(bs, ls, fh))
        outs["ss"].append(s_s)

    y_prompt = _final_norm(h, ln_final, row0=0, rows=np_rows, tm=tm).reshape(bp, lp, d)
    y_sample = _final_norm(h, ln_final, row0=np_rows, rows=ns_rows, tm=tm).reshape(bs, ls, d)
    st = {k: jnp.stack(v) for k, v in outs.items()}
    kp, vp, ks, vs = kv_out
    return (y_prompt, y_sample, kp.reshape(depth, bp, lp, fh, fdh), vp.reshape(depth, bp, lp, fh, fdh),
            st["lfp"], st["sp"], ks.reshape(depth, bs, ls, fh, fdh), vs.reshape(depth, bs, ls, fh, fdh),
            st["lfs"], st["ss"])
```

```python
import functools

import jax
import jax.numpy as jnp
from jax import lax
from jax.experimental import pallas as pl
from jax.experimental.pallas import tpu as pltpu

EPS = 1e-6
MASK_VALUE = -1e30
N_MOD = 9
LANES = 128
SUBLANES = 8
V7X_VMEM_BYTES = 64 * 1024 * 1024
VMEM_LIMIT = V7X_VMEM_BYTES * 3 // 4
HGRN_SUB = 16
LOG2E = 1.4426950408889634

F32 = jnp.float32
BF16 = jnp.bfloat16


def _params(*sem):
    return pltpu.CompilerParams(dimension_semantics=sem, vmem_limit_bytes=VMEM_LIMIT)


def _sigmoid(x):
    return 1.0 / (1.0 + jnp.exp(-x))


def _dot(a, b):
    return jnp.dot(a, b, preferred_element_type=F32)


def _dot_nt(a, b):
    return lax.dot_general(a, b, (((1,), (1,)), ((), ())), preferred_element_type=F32)


def _dot_tn(a, b):
    return lax.dot_general(a, b, (((0,), (0,)), ((), ())), preferred_element_type=F32)


def _split3(x):
    x1 = x.astype(BF16)
    r1 = x - x1.astype(F32)
    x2 = r1.astype(BF16)
    x3 = (r1 - x2.astype(F32)).astype(BF16)
    return x1, x2, x3


def _pick(is_prompt, p_ref, s_ref):
    return jnp.where(is_prompt, p_ref[...][None], s_ref[...])


def _mod_specs(k, npt, gt, n, col=None):
    if col is None:
        p = pl.BlockSpec((None, None, 1, n), lambda i, *_: (0, k, 0, 0))
        s = pl.BlockSpec((gt, None, 1, n), lambda i, *_: (jnp.maximum(i - npt, 0), k, 0, 0))
    else:
        p = pl.BlockSpec((None, None, 1, n), lambda i, j: (0, k, 0, j))
        s = pl.BlockSpec((gt, None, 1, n), lambda i, j: (jnp.maximum(i - npt, 0), k, 0, j))
    return p, s


def _ada_kernel(c_ref, w_ref, b_ref, o_ref):
    c = c_ref[...]
    a = (c * _sigmoid(c)).astype(BF16)
    o_ref[...] = _dot(a, w_ref[...].astype(BF16)) + b_ref[...]


def _ada(c_pad, w, b, layer):
    m, d = c_pad.shape
    depth, _, n = w.shape
    tn = 1024 if n % 1024 == 0 else n
    return pl.pallas_call(
        _ada_kernel,
        grid=(n // tn,),
        in_specs=[pl.BlockSpec((m, d), lambda j: (0, 0)),
                  pl.BlockSpec((None, d, tn), lambda j: (layer, 0, j)),
                  pl.BlockSpec((None, 1, tn), lambda j: (layer, 0, j))],
        out_specs=pl.BlockSpec((m, tn), lambda j: (0, j)),
        out_shape=jax.ShapeDtypeStruct((m, n), F32),
        compiler_params=_params("arbitrary"),
        name="ada",
    )(c_pad, w, b.reshape(depth, 1, n))


def _normmod_kernel(x_ref, g_ref, shp_ref, shs_ref, scp_ref, scs_ref, o_ref, *, npt, gt):
    is_p = pl.program_id(0) < npt
    x = x_ref[...]
    tm, d = x.shape
    y = x * lax.rsqrt(jnp.mean(x * x, axis=-1, keepdims=True) + EPS) * g_ref[...]
    sh = _pick(is_p, shp_ref, shs_ref)
    sc = _pick(is_p, scp_ref, scs_ref)
    y = y.reshape(gt, tm // gt, d) * (1.0 + sc) + sh
    o_ref[...] = y.reshape(tm, d).astype(o_ref.dtype)


def _normmod(x, g, mod_p, mod_s, k_shift, k_scale, *, tm, npt, gt):
    t, d = x.shape
    shp, shs = _mod_specs(k_shift, npt, gt, d)
    scp, scs = _mod_specs(k_scale, npt, gt, d)
    return pl.pallas_call(
        functools.partial(_normmod_kernel, npt=npt, gt=gt),
        grid=(t // tm,),
        in_specs=[pl.BlockSpec((tm, d), lambda i: (i, 0)),
                  pl.BlockSpec((1, d), lambda i: (0, 0)),
                  shp, shs, scp, scs],
        out_specs=pl.BlockSpec((tm, d), lambda i: (i, 0)),
        out_shape=jax.ShapeDtypeStruct((t, d), BF16),
        compiler_params=_params("parallel"),
        name="normmod",
    )(x, g.reshape(1, d), mod_p, mod_s, mod_p, mod_s)


def _rmsnorm_kernel(x_ref, g_ref, o_ref):
    x = x_ref[...]
    o_ref[...] = x * lax.rsqrt(jnp.mean(x * x, axis=-1, keepdims=True) + EPS) * g_ref[...]


def _final_norm(x, g, *, row0, rows, tm):
    d = x.shape[1]
    b0 = row0 // tm
    return pl.pallas_call(
        _rmsnorm_kernel,
        grid=(rows // tm,),
        in_specs=[pl.BlockSpec((tm, d), lambda i: (b0 + i, 0)),
                  pl.BlockSpec((1, d), lambda i: (0, 0))],
        out_specs=pl.BlockSpec((tm, d), lambda i: (i, 0)),
        out_shape=jax.ShapeDtypeStruct((rows, d), F32),
        compiler_params=_params("parallel"),
        name="final_norm",
    )(x, g.reshape(1, d))


def _ffn_kernel(n_ref, x_ref, wg_ref, wu_ref, wd_ref, gp_ref, gs_ref, o_ref, acc_ref, *, npt, gt):
    i, j = pl.program_id(0), pl.program_id(1)

    @pl.when(j == 0)
    def _():
        acc_ref[...] = jnp.zeros_like(acc_ref)

    n = n_ref[...]
    a = _dot(n, wg_ref[...])
    b = _dot(n, wu_ref[...])
    acc_ref[...] += _dot((a * _sigmoid(a) * b).astype(BF16), wd_ref[...])

    @pl.when(j == pl.num_programs(1) - 1)
    def _():
        tm, d = acc_ref.shape
        gate = _pick(i < npt, gp_ref, gs_ref)
        y = (0.5 * gate) * acc_ref[...].reshape(gt, tm // gt, d)
        o_ref[...] = x_ref[...] + y.reshape(tm, d)


def _ffn(n, x, wg, wu, wd, mod_p, mod_s, k_gate, *, tm, npt, gt):
    t, d = x.shape
    f = wg.shape[1]
    tf = 512 if f % 512 == 0 else f
    gp, gs = _mod_specs(k_gate, npt, gt, d)
    return pl.pallas_call(
        functools.partial(_ffn_kernel, npt=npt, gt=gt),
        grid=(t // tm, f // tf),
        in_specs=[pl.BlockSpec((tm, d), lambda i, j: (i, 0)),
                  pl.BlockSpec((tm, d), lambda i, j: (i, 0)),
                  pl.BlockSpec((d, tf), lambda i, j: (0, j)),
                  pl.BlockSpec((d, tf), lambda i, j: (0, j)),
                  pl.BlockSpec((tf, d), lambda i, j: (j, 0)),
                  gp, gs],
        out_specs=pl.BlockSpec((tm, d), lambda i, j: (i, 0)),
        out_shape=jax.ShapeDtypeStruct((t, d), F32),
        scratch_shapes=[pltpu.VMEM((tm, d), F32)],
        compiler_params=_params("parallel", "arbitrary"),
        name="ffn",
    )(n, x, wg, wu, wd, mod_p, mod_s)


def _mm_kernel(x_ref, w_ref, *o_refs):
    r = _dot(x_ref[...], w_ref[...])
    for o_ref in o_refs:
        o_ref[...] = r.astype(o_ref.dtype)


def _mm(x, w, out_dtypes, *, tm, tn):
    t, d = x.shape
    n = w.shape[1]
    outs = pl.pallas_call(
        _mm_kernel,
        grid=(t // tm, n // tn),
        in_specs=[pl.BlockSpec((tm, d), lambda i, j: (i, 0)),
                  pl.BlockSpec((d, tn), lambda i, j: (0, j))],
        out_specs=[pl.BlockSpec((tm, tn), lambda i, j: (i, j)) for _ in out_dtypes],
        out_shape=[jax.ShapeDtypeStruct((t, n), dt) for dt in out_dtypes],
        compiler_params=_params("parallel", "arbitrary"),
        name="proj",
    )(x, w)
    return outs


def _kvproj_kernel(x_ref, w_ref, *refs, npt, heads, ct, has_prev):
    kv16_ref, kp_ref, vp_ref, ks_ref, vs_ref = refs[4:] if has_prev else refs
    is_p = pl.program_id(0) < npt
    x = x_ref[...]
    tm = x.shape[0]
    fw = heads * LANES
    for dst_p, dst_s, c0 in ((kp_ref, ks_ref, 0), (vp_ref, vs_ref, fw)):
        for cj in range(fw // ct):
            cols = slice(c0 + cj * ct, c0 + (cj + 1) * ct)
            r = _dot(x, w_ref[:, cols])
            kv16_ref[:, cols] = r.astype(kv16_ref.dtype)

            def store(dst_ref, r=r, cj=cj):
                for hh in range(ct // LANES):
                    dst_ref[pl.ds(cj * (ct // LANES) + hh, tm, stride=heads), :] = (
                        r[:, hh * LANES:(hh + 1) * LANES])

            pl.when(is_p)(functools.partial(store, dst_p))
            pl.when(jnp.logical_not(is_p))(functools.partial(store, dst_s))


def _kvproj(x, w, prev, *, layer, depth, heads, np_rows, tm):
    t, d = x.shape
    n = w.shape[1]
    npt = np_rows // tm
    ns_rows = t - np_rows
    fw = heads * LANES
    ct = 512 if fw % 512 == 0 else LANES
    blk = (None, tm * heads, LANES)
    p_spec = pl.BlockSpec(blk, lambda i: (layer, jnp.minimum(i, npt - 1), 0))
    s_spec = pl.BlockSpec(blk, lambda i: (layer, jnp.maximum(i - npt, 0), 0))
    p_shape = jax.ShapeDtypeStruct((depth, np_rows * heads, LANES), F32)
    s_shape = jax.ShapeDtypeStruct((depth, ns_rows * heads, LANES), F32)
    has_prev = prev is not None
    any_spec = pl.BlockSpec(memory_space=pl.ANY)
    return pl.pallas_call(
        functools.partial(_kvproj_kernel, npt=npt, heads=heads, ct=ct, has_prev=has_prev),
        grid=(t // tm,),
        in_specs=[pl.BlockSpec((tm, d), lambda i: (i, 0)),
                  pl.BlockSpec((d, n), lambda i: (0, 0), pipeline_mode=pl.Buffered(1))]
        + ([any_spec] * 4 if has_prev else []),
        out_specs=[pl.BlockSpec((tm, n), lambda i: (i, 0)), p_spec, p_spec, s_spec, s_spec],
        out_shape=[jax.ShapeDtypeStruct((t, n), BF16), p_shape, p_shape, s_shape, s_shape],
        input_output_aliases={2: 1, 3: 2, 4: 3, 5: 4} if has_prev else {},
        compiler_params=_params("arbitrary"),
        name="kvproj",
    )(x, w, *(prev if has_prev else ()))


def _mm_logsig_kernel(x_ref, w_ref, b_ref, o_ref):
    y = _dot(x_ref[...], w_ref[...]) + b_ref[...]
    o_ref[...] = jnp.minimum(y, 0.0) - jnp.log(1.0 + jnp.exp(-jnp.abs(y)))


def _mm_logsig(x, w, b, *, tm):
    t, d = x.shape
    n = w.shape[1]
    return pl.pallas_call(
        _mm_logsig_kernel,
        grid=(t // tm,),
        in_specs=[pl.BlockSpec((tm, d), lambda i: (i, 0)),
                  pl.BlockSpec((d, n), lambda i: (0, 0)),
                  pl.BlockSpec((1, n), lambda i: (0, 0))],
        out_specs=pl.BlockSpec((tm, n), lambda i: (i, 0)),
        out_shape=jax.ShapeDtypeStruct((t, n), F32),
        compiler_params=_params("parallel"),
        name="fox_logf",
    )(x, w, b)


def _merge_kernel(o_ref, y_ref, wa_ref, wb_ref, ga_ref, gb_ref, out_ref):
    a = _dot(o_ref[...], wa_ref[...])
    b = _dot(y_ref[...], wb_ref[...])
    out_ref[...] = (_sigmoid(ga_ref[...]) * a + _sigmoid(gb_ref[...]) * b).astype(out_ref.dtype)


def _merge(o, y, wa, wb, p32, col_ga, col_gb, *, tm, tn):
    t, hw = o.shape
    fw = y.shape[1]
    d = wa.shape[1]
    ja, jb = col_ga // tn, col_gb // tn
    return pl.pallas_call(
        _merge_kernel,
        grid=(t // tm, d // tn),
        in_specs=[pl.BlockSpec((tm, hw), lambda i, j: (i, 0)),
                  pl.BlockSpec((tm, fw), lambda i, j: (i, 0)),
                  pl.BlockSpec((hw, tn), lambda i, j: (0, j)),
                  pl.BlockSpec((fw, tn), lambda i, j: (0, j)),
                  pl.BlockSpec((tm, tn), lambda i, j: (i, ja + j)),
                  pl.BlockSpec((tm, tn), lambda i, j: (i, jb + j))],
        out_specs=pl.BlockSpec((tm, tn), lambda i, j: (i, j)),
        out_shape=jax.ShapeDtypeStruct((t, d), BF16),
        compiler_params=_params("parallel", "arbitrary"),
        name="merge",
    )(o, y, wa, wb, p32, p32)


def _outproj_kernel(m_ref, w_ref, x_ref, gp_ref, gs_ref, o_ref, *, npt, gt):
    r = _dot(m_ref[...], w_ref[...])
    tm, tn = r.shape
    gate = _pick(pl.program_id(0) < npt, gp_ref, gs_ref)
    y = gate * r.reshape(gt, tm // gt, tn)
    o_ref[...] = x_ref[...] + y.reshape(tm, tn)


def _outproj(m, w, x, mod_p, mod_s, k_gate, *, tm, tn, npt, gt):
    t, d = x.shape
    gp, gs = _mod_specs(k_gate, npt, gt, tn, col=True)
    return pl.pallas_call(
        functools.partial(_outproj_kernel, npt=npt, gt=gt),
        grid=(t // tm, d // tn),
        in_specs=[pl.BlockSpec((tm, d), lambda i, j: (i, 0)),
                  pl.BlockSpec((d, tn), lambda i, j: (0, j)),
                  pl.BlockSpec((tm, tn), lambda i, j: (i, j)),
                  gp, gs],
        out_specs=pl.BlockSpec((tm, tn), lambda i, j: (i, j)),
        out_shape=jax.ShapeDtypeStruct((t, d), F32),
        compiler_params=_params("parallel", "arbitrary"),
        name="outproj",
    )(m, w, x, mod_p, mod_s)


def _per_head(fn, hb):
    return jnp.concatenate([fn(slice(h * LANES, (h + 1) * LANES)) for h in range(hb)], axis=1)


def _hgrn_heads(q16, z, v16, ag, sts, lg, hgn, *, layer):
    C, W = z.shape
    hb = W // LANES
    sub = min(HGRN_SUB, C)

    def head_sum(x):
        return _per_head(lambda ln: jnp.broadcast_to(
            jnp.sum(x[:, ln], axis=-1, keepdims=True), (x.shape[0], LANES)), hb)

    e = jnp.exp(lg - jnp.max(lg, axis=0, keepdims=True))
    soft = e / jnp.sum(e, axis=0, keepdims=True)
    lb = jnp.zeros((1, W), F32)
    for i in range(1, layer + 1):
        lb = lb + soft[i:i + 1]

    ez = jnp.exp(-jnp.abs(z))
    r = 1.0 / (1.0 + ez)
    sig_pos = jnp.where(z >= 0, r, ez * r)
    sig_neg = jnp.where(z >= 0, ez * r, r)
    g = jnp.log(lb + (1.0 - lb) * sig_pos) * LOG2E
    kin = (1.0 - lb) * sig_neg

    row = lax.broadcasted_iota(jnp.int32, (C, C), 0)
    col = lax.broadcasted_iota(jnp.int32, (C, C), 1)
    tri = (row >= col).astype(BF16)
    g1, g2, g3 = _split3(g)
    cum = _dot(tri, g1) + _dot(tri, g2) + _dot(tri, g3)
    cum_k = cum - jnp.log(kin) * LOG2E

    q = q16.astype(F32)
    v = v16.astype(F32)
    qe_all = (q * jnp.exp2(cum)).astype(BF16)
    sts16 = [st.astype(BF16) for st in sts]
    nsub = C // sub
    o_inter = _per_head(lambda ln: _dot_nt(qe_all[:, ln], sts16[ln.start // LANES]), hb)
    att = {}
    for i in range(1, nsub):
        r0 = i * sub
        b_i = cum[r0 - 1:r0]
        qe = (q[r0:r0 + sub] * jnp.exp2(cum[r0:r0 + sub] - b_i)).astype(BF16)
        ke = (kin[:r0] * jnp.exp2(b_i - cum[:r0])).astype(BF16)
        for h in range(hb):
            ln = slice(h * LANES, (h + 1) * LANES)
            att[i, h] = _dot_nt(qe[:, ln], ke[:, ln]).astype(BF16)
    o_off = {i: _per_head(lambda ln: _dot(att[i, ln.start // LANES], v16[:i * sub, ln]), hb)
             for i in range(1, nsub)}

    row8 = lax.broadcasted_iota(jnp.int32, (SUBLANES, 1), 0)
    outs = []
    for i in range(nsub):
        r0 = i * sub
        acc = o_inter[r0:r0 + sub]
        if i > 0:
            acc = acc + o_off[i]
        for u in range(sub // SUBLANES):
            t0 = r0 + u * SUBLANES
            cum_t, q_t = cum[t0:t0 + SUBLANES], q[t0:t0 + SUBLANES]
            acc_t = acc[u * SUBLANES:(u + 1) * SUBLANES]
            for s in range(r0, t0 + SUBLANES):
                d = cum_t - cum_k[s:s + 1]
                if s >= t0:
                    d = jnp.where(row8 >= s - t0, d, MASK_VALUE)
                acc_t = acc_t + head_sum(q_t * jnp.exp2(d)) * v[s:s + 1]
            outs.append(acc_t)
    o = jnp.concatenate(outs, axis=0)

    o = o * lax.rsqrt(head_sum(o * o) * (1.0 / LANES) + EPS) * hgn
    agf = ag.astype(F32)
    o = o * (agf * _sigmoid(agf))

    last = cum[C - 1:C]
    kdec = (kin * jnp.exp2(last - cum)).astype(BF16)
    decay = jnp.exp2(last)
    new_sts = [sts[h] * decay[:, h * LANES:(h + 1) * LANES]
               + _dot_tn(v16[:, h * LANES:(h + 1) * LANES], kdec[:, h * LANES:(h + 1) * LANES])
               for h in range(hb)]
    return o, new_sts


def _hgrn_kernel(q_ref, z_ref, v_ref, ag_ref, s0_ref, lbl_ref, hgn_ref, o_ref, sout_ref, st_ref,
                 *, layer, nc, hb):
    c = pl.program_id(2)

    @pl.when(c == 0)
    def _():
        for hh in range(hb):
            st_ref[hh] = s0_ref[hh].T

    o, new_sts = _hgrn_heads(q_ref[...], z_ref[...], v_ref[...], ag_ref[...],
                             [st_ref[hh] for hh in range(hb)], lbl_ref[...], hgn_ref[...], layer=layer)
    o_ref[...] = o.astype(o_ref.dtype)
    for hh in range(hb):
        st_ref[hh] = new_sts[hh]

    @pl.when(c == nc - 1)
    def _():
        for hh in range(hb):
            sout_ref[hh] = st_ref[hh].T


def _hgrn(p16, p32, s0, lb_logits, hg_norm, *, layer, row0, nb, length, chunk, hb,
          col_q, col_v, col_z, col_ag):
    heads = s0.shape[1]
    nc = length // chunk
    rb0 = row0 // chunk
    depth = lb_logits.shape[0]
    w = hb * LANES

    def rows(b, h, c):
        return rb0 + b * nc + c

    return pl.pallas_call(
        functools.partial(_hgrn_kernel, layer=layer, nc=nc, hb=hb),
        grid=(nb, heads // hb, nc),
        in_specs=[pl.BlockSpec((chunk, w), lambda b, h, c: (rows(b, h, c), col_q // hb + h)),
                  pl.BlockSpec((chunk, w), lambda b, h, c: (rows(b, h, c), col_z // hb + h)),
                  pl.BlockSpec((chunk, w), lambda b, h, c: (rows(b, h, c), col_v // hb + h)),
                  pl.BlockSpec((chunk, w), lambda b, h, c: (rows(b, h, c), col_ag // hb + h)),
                  pl.BlockSpec((None, hb, LANES, LANES), lambda b, h, c: (b, h, 0, 0)),
                  pl.BlockSpec((depth, w), lambda b, h, c: (0, h)),
                  pl.BlockSpec((1, w), lambda b, h, c: (0, h))],
        out_specs=[pl.BlockSpec((chunk, w), lambda b, h, c: (b * nc + c, h)),
                   pl.BlockSpec((None, hb, LANES, LANES), lambda b, h, c: (b, h, 0, 0))],
        out_shape=[jax.ShapeDtypeStruct((nb * length, heads * LANES), BF16),
                   jax.ShapeDtypeStruct((nb, heads, LANES, LANES), F32)],
        scratch_shapes=[pltpu.VMEM((hb, LANES, LANES), F32)],
        compiler_params=_params("parallel", "parallel", "arbitrary"),
        name="hgrn",
    )(p16, p32, p16, p32, s0, lb_logits, hg_norm)


def _cumsum_kernel(x_ref, o_ref, carry_ref):
    @pl.when(pl.program_id(0) == 0)
    def _():
        carry_ref[...] = jnp.zeros_like(carry_ref)

    x = x_ref[...]
    bw = x.shape[1]
    row = lax.broadcasted_iota(jnp.int32, (bw, bw), 0)
    col = lax.broadcasted_iota(jnp.int32, (bw, bw), 1)
    upper = (row <= col).astype(BF16)
    x1, x2, x3 = _split3(x)
    c = _dot(x1, upper) + _dot(x2, upper) + _dot(x3, upper) + carry_ref[...]
    o_ref[...] = c
    carry_ref[...] = c[:, bw - 1:bw]


def _cumsum_lanes(x):
    r, n = x.shape
    bw = 256 if n % 256 == 0 else LANES
    return pl.pallas_call(
        _cumsum_kernel,
        grid=(n // bw,),
        in_specs=[pl.BlockSpec((r, bw), lambda j: (0, j))],
        out_specs=pl.BlockSpec((r, bw), lambda j: (0, j)),
        out_shape=jax.ShapeDtypeStruct((r, n), F32),
        scratch_shapes=[pltpu.VMEM((r, 1), F32)],
        compiler_params=_params("arbitrary"),
        name="cumsum",
    )(x)


def _fox_update(s, v, m_ref, l_ref, acc_ref, slot, masked):
    tq, tk = s.shape
    if masked:
        row = lax.broadcasted_iota(jnp.int32, (tq, tk), 0)
        col = lax.broadcasted_iota(jnp.int32, (tq, tk), 1)
        s = jnp.where(row >= col, s, MASK_VALUE)
    m_old = m_ref[slot]
    m_new = jnp.maximum(m_old, jnp.max(s, axis=-1, keepdims=True))
    alpha = jnp.exp2(m_old - m_new)
    if tk % LANES == 0:
        ps = [jnp.exp2(s[:, c * LANES:(c + 1) * LANES] - m_new) for c in range(tk // LANES)]
        p_lanes = functools.reduce(jnp.add, ps)
        p = jnp.concatenate(ps, axis=1)
    else:
        p = jnp.exp2(s - m_new[:, :tk])
        p_lanes = jnp.concatenate([p, jnp.zeros((tq, LANES - tk), F32)], axis=1)
    l_ref[slot] = alpha * l_ref[slot] + p_lanes
    acc_ref[slot] = alpha * acc_ref[slot] + _dot(p.astype(BF16), v)
    m_ref[slot] = m_new


def _fox_init(m_ref, l_ref, acc_ref):
    m_ref[...] = jnp.full_like(m_ref, MASK_VALUE)
    l_ref[...] = jnp.zeros_like(l_ref)
    acc_ref[...] = jnp.zeros_like(acc_ref)


def _fox_out(l_ref, acc_ref, slot):
    return acc_ref[slot] / jnp.sum(l_ref[slot], axis=-1, keepdims=True)


def _fox_tile(q, k, v, bias, m_ref, l_ref, acc_ref, slot, masked):
    _fox_update(_dot_nt(q, k) + bias, v, m_ref, l_ref, acc_ref, slot, masked)


def _fox_prompt_kernel(q_ref, k_ref, v_ref, c_ref, o_ref, s_ref, m_ref, l_ref, acc_ref, *, tb):
    qi = pl.program_id(1)
    _fox_init(m_ref, l_ref, acc_ref)
    base = pl.multiple_of(qi * (2 * tb), 2 * tb)
    c_first = c_ref[:, pl.ds(base, LANES)][:, 0:1]

    def scores(half, slot, kstart):
        bias = (c_first - c_ref[:, pl.ds(kstart, tb)]) * LOG2E
        s_ref[half, slot] = _dot_nt(q_ref[half * tb:(half + 1) * tb], k_ref[pl.ds(kstart, tb), :]) + bias

    def update(half, slot, kstart, masked):
        _fox_update(s_ref[half, slot], v_ref[pl.ds(kstart, tb), :], m_ref, l_ref, acc_ref, half, masked)

    scores(0, 0, 0)
    scores(1, 0, 0)

    def body(j, carry):
        k0 = pl.multiple_of(j * (2 * tb), 2 * tb)
        k1 = pl.multiple_of(k0 + tb, tb)
        k2 = pl.multiple_of(k0 + 2 * tb, tb)
        for half in range(2):
            scores(half, 1, k1)
            update(half, 0, k0, False)
        for half in range(2):
            scores(half, 0, k2)
            update(half, 1, k1, False)
        return carry

    lax.fori_loop(0, qi, body, 0)
    top = pl.multiple_of(base + tb, tb)
    scores(1, 1, top)
    update(0, 0, base, True)
    update(1, 0, base, False)
    update(1, 1, top, True)
    for half in range(2):
        o_ref[half * tb:(half + 1) * tb] = _fox_out(l_ref, acc_ref, half).astype(o_ref.dtype)


def _fox_prompt(p16, kv16, c, *, heads, length, col_q, col_k, col_v, tb):
    return pl.pallas_call(
        functools.partial(_fox_prompt_kernel, tb=tb),
        grid=(heads, length // (2 * tb)),
        in_specs=[pl.BlockSpec((2 * tb, LANES), lambda h, qi: (qi, col_q + h)),
                  pl.BlockSpec((length, LANES), lambda h, qi: (0, col_k + h)),
                  pl.BlockSpec((length, LANES), lambda h, qi: (0, col_v + h)),
                  pl.BlockSpec((None, 1, length), lambda h, qi: (h, 0, 0))],
        out_specs=pl.BlockSpec((2 * tb, LANES), lambda h, qi: (qi, h)),
        out_shape=jax.ShapeDtypeStruct((length, heads * LANES), BF16),
        scratch_shapes=[pltpu.VMEM((2, 2, tb, tb), F32), pltpu.VMEM((2, tb, LANES), F32),
                        pltpu.VMEM((2, tb, LANES), F32), pltpu.VMEM((2, tb, LANES), F32)],
        compiler_params=_params("parallel", "arbitrary"),
        name="fox_prompt",
    )(p16, kv16, kv16, c)


def _fox_sample_kernel(q_ref, kn_ref, vn_ref, kc_ref, vc_ref, cq_ref, ckc_ref, ckn_ref, o_ref,
                       m_ref, l_ref, acc_ref, *, heads, nck):
    c = pl.program_id(1)
    tkc = kc_ref.shape[0] // heads

    @pl.when(c == 0)
    def _():
        _fox_init(m_ref, l_ref, acc_ref)

    cq_all = cq_ref[...] * LOG2E

    def head_lanes(h):
        return slice(h * LANES, (h + 1) * LANES)

    for h in range(heads):
        k = kc_ref[pl.ds(h, tkc, stride=heads), :].astype(BF16)
        v = vc_ref[pl.ds(h, tkc, stride=heads), :].astype(BF16)
        bias = cq_all[:, h:h + 1] - ckc_ref[h] * LOG2E
        _fox_tile(q_ref[:, head_lanes(h)], k, v, bias, m_ref, l_ref, acc_ref, h, False)

    @pl.when(c == nck - 1)
    def _():
        for h in range(heads):
            bias = cq_all[:, h:h + 1] - ckn_ref[h] * LOG2E
            _fox_tile(q_ref[:, head_lanes(h)], kn_ref[:, head_lanes(h)], vn_ref[:, head_lanes(h)],
                      bias, m_ref, l_ref, acc_ref, h, True)
            o_ref[:, head_lanes(h)] = _fox_out(l_ref, acc_ref, h).astype(o_ref.dtype)


def _fox_sample(p16, kv16, cache_k, cache_v, cq, ckc, ckn, *, layer, row0, nb, ls, heads,
                col_q, col_k, col_v, tkc):
    past = cache_k.shape[2] // heads
    nck = past // tkc
    rb0 = row0 // ls
    w = heads * LANES
    stat = pltpu.VMEM((heads, ls, LANES), F32)
    return pl.pallas_call(
        functools.partial(_fox_sample_kernel, heads=heads, nck=nck),
        grid=(nb, nck),
        in_specs=[pl.BlockSpec((ls, w), lambda b, c: (rb0 + b, col_q // heads)),
                  pl.BlockSpec((ls, w), lambda b, c: (rb0 + b, col_k // heads)),
                  pl.BlockSpec((ls, w), lambda b, c: (rb0 + b, col_v // heads)),
                  pl.BlockSpec((None, None, tkc * heads, LANES), lambda b, c: (layer, b, c, 0)),
                  pl.BlockSpec((None, None, tkc * heads, LANES), lambda b, c: (layer, b, c, 0)),
                  pl.BlockSpec((None, ls, heads), lambda b, c: (b, 0, 0)),
                  pl.BlockSpec((None, heads, 1, tkc), lambda b, c: (b, 0, 0, c)),
                  pl.BlockSpec((None, heads, 1, ls), lambda b, c: (b, 0, 0, 0))],
        out_specs=pl.BlockSpec((ls, w), lambda b, c: (b, 0)),
        out_shape=jax.ShapeDtypeStruct((nb * ls, w), BF16),
        scratch_shapes=[stat, stat, stat],
        compiler_params=_params("parallel", "arbitrary"),
        name="fox_sample",
    )(p16, kv16, kv16, cache_k, cache_v, cq, ckc, ckn)


def _row_tile(np_rows, ns_rows, ls):
    for tm in (512, 256, 128, 64, 32):
        if np_rows % tm == 0 and ns_rows % tm == 0 and tm % ls == 0:
            return tm
    raise ValueError("no row tile fits the prompt / sample row counts")


def kernel(x_prompt, x_sample, c_prompt, c_sample, cache_fox_k, cache_fox_v, cache_fox_logf,
           state_hgrn, ada_w, ada_b, ln_ffn1, ln_mix, ln_ffn2, ffn1_w_gate, ffn1_w_up,
           ffn1_w_down, ffn2_w_gate, ffn2_w_up, ffn2_w_down, w_in, hgrn_lb_logits,
           hgrn_norm_g, fox_f_bias, w_branch_a, w_branch_b, w_out, ln_final):
    bp, lp, d = x_prompt.shape
    bs, ls, _ = x_sample.shape
    depth = ada_w.shape[0]
    _, _, past, fh, fdh = cache_fox_k.shape
    _, _, hh, dk, dv = state_hgrn.shape
    assert bp == 1 and dk == LANES and dv == LANES and fdh == LANES
    assert ls % SUBLANES == 0 and d % LANES == 0
    np_rows, ns_rows = bp * lp, bs * ls
    t = np_rows + ns_rows
    hw, fw = hh * LANES, fh * LANES
    tm = _row_tile(np_rows, ns_rows, ls)
    npt, gt = np_rows // tm, tm // ls
    tn = 512 if (d % 512 == 0 and hw % 512 == 0 and fw % 512 == 0) else LANES
    chunk_p = 64 if lp % 64 == 0 else lp
    tb = 512 if lp % 1024 == 0 else lp // 2
    tkc = 512 if past % 512 == 0 else past
    hb = next(n for n in (8, 4, 2, 1) if hh % n == 0)
    tm_kv = min(tm, 256)
    tm_big = 2 * tm if (np_rows % (2 * tm) == 0 and ns_rows % (2 * tm) == 0) else tm
    tiles = dict(tm=tm, npt=npt, gt=gt)
    tiles_big = dict(tm=tm_big, npt=np_rows // tm_big, gt=tm_big // ls)
    q_scale = LOG2E * float(fdh) ** -0.5

    x = jnp.concatenate([x_prompt.reshape(np_rows, d), x_sample.reshape(ns_rows, d)], axis=0)
    c_all = jnp.concatenate([c_prompt, c_sample], axis=0)
    m_pad = -(-c_all.shape[0] // SUBLANES) * SUBLANES
    c_pad = jnp.pad(c_all, ((0, m_pad - c_all.shape[0]), (0, 0)))
    s0_prompt = jnp.zeros((bp, hh, dk, dv), F32)

    sizes = (hw, hw, hw, hw, fw, fw, fw, fh, d, d)
    offs = [0]
    for s in sizes:
        offs.append(offs[-1] + s)
    o_aq, o_af, o_ai, o_ag, o_fq, o_fk, o_fv, o_ffl, o_ga, o_gb = offs[:10]

    outs = {k: [] for k in ("lfp", "sp", "lfs", "ss")}
    cache_k = cache_fox_k.reshape(depth, bs, past * fh, fdh)
    cache_v = cache_fox_v.reshape(depth, bs, past * fh, fdh)
    kv_out = None
    h = x
    for l in range(depth):
        wi = w_in[l]
        w16 = jnp.concatenate([wi[:, o_aq:o_aq + hw], wi[:, o_ai:o_ai + hw],
                               wi[:, o_fq:o_fq + fw] * q_scale], axis=1).astype(BF16)
        w32 = jnp.concatenate([wi[:, o_af:o_af + hw], wi[:, o_ag:o_ag + hw],
                               wi[:, o_ga:o_ga + d], wi[:, o_gb:o_gb + d]], axis=1).astype(BF16)
        wkv = wi[:, o_fk:o_fk + 2 * fw].astype(BF16)
        wffl = jnp.pad(wi[:, o_ffl:o_ffl + fh], ((0, 0), (0, LANES - fh))).astype(BF16)
        bffl = jnp.pad(fox_f_bias[l], (0, LANES - fh)).reshape(1, LANES)

        mod = _ada(c_pad, ada_w, ada_b, l)
        mod_p = mod[0:bp].reshape(bp, N_MOD, 1, d)
        mod_s = mod[bp:bp + bs].reshape(bs, N_MOD, 1, d)

        n1 = _normmod(h, ln_ffn1[l], mod_p, mod_s, 0, 1, **tiles)
        h = _ffn(n1, h, ffn1_w_gate[l].astype(BF16), ffn1_w_up[l].astype(BF16),
                 ffn1_w_down[l].astype(BF16), mod_p, mod_s, 2, **tiles)

        n2 = _normmod(h, ln_mix[l], mod_p, mod_s, 3, 4, **tiles)
        (p16,) = _mm(n2, w16, (BF16,), tm=tm_big, tn=tn)
        (p32,) = _mm(n2, w32, (F32,), tm=tm_big, tn=tn)
        kv16, *kv_out = _kvproj(n2, wkv, kv_out, layer=l, depth=depth, heads=fh,
                                np_rows=np_rows, tm=tm_kv)
        lf_pad = _mm_logsig(n2, wffl, bffl, tm=tm)

        cb = lambda off: off // LANES
        hg_cols = dict(col_q=0, col_v=cb(hw), col_z=0, col_ag=cb(hw), hb=hb)
        o_p, s_p = _hgrn(p16, p32, s0_prompt, hgrn_lb_logits, hgrn_norm_g[l].reshape(1, hw),
                         layer=l, row0=0, nb=bp, length=lp, chunk=chunk_p, **hg_cols)
        o_s, s_s = _hgrn(p16, p32, state_hgrn[l], hgrn_lb_logits, hgrn_norm_g[l].reshape(1, hw),
                         layer=l, row0=np_rows, nb=bs, length=ls, chunk=ls, **hg_cols)

        lf = lf_pad[:, :fh]
        lf_p, lf_s = lf[:np_rows], lf[np_rows:]
        c_p = _cumsum_lanes(lf_p.T)
        cache_t = cache_fox_logf[l].transpose(0, 2, 1).reshape(bs * fh, past)
        new_t = lf_s.reshape(bs, ls, fh).transpose(0, 2, 1).reshape(bs * fh, ls)
        n_all = -(-(past + ls) // LANES) * LANES
        c_s = _cumsum_lanes(jnp.concatenate(
            [cache_t, new_t, jnp.zeros((bs * fh, n_all - past - ls), F32)], axis=1))
        cq_s = c_s[:, past:past + ls].reshape(bs, fh, ls).transpose(0, 2, 1)
        ckc_s = c_s[:, :past].reshape(bs, fh, 1, past)
        ckn_s = c_s[:, past:past + ls].reshape(bs, fh, 1, ls)

        fox_cols = dict(col_q=cb(2 * hw), col_k=0, col_v=cb(fw))
        y_p = _fox_prompt(p16, kv16, c_p.reshape(fh, 1, lp), heads=fh, length=lp, tb=tb, **fox_cols)
        y_s = _fox_sample(p16, kv16, cache_k, cache_v, cq_s, ckc_s, ckn_s, layer=l,
                          row0=np_rows, nb=bs, ls=ls, heads=fh, tkc=tkc, **fox_cols)

        o_all = jnp.concatenate([o_p, o_s], axis=0)
        y_all = jnp.concatenate([y_p, y_s], axis=0)
        merged = _merge(o_all, y_all, w_branch_a[l].astype(BF16), w_branch_b[l].astype(BF16),
                        p32, 2 * hw, 2 * hw + d, tm=tm_big, tn=tn)
        h = _outproj(merged, w_out[l].astype(BF16), h, mod_p, mod_s, 5, tn=tn, **tiles_big)

        n3 = _normmod(h, ln_ffn2[l], mod_p, mod_s, 6, 7, **tiles)
        h = _ffn(n3, h, ffn2_w_gate[l].astype(BF16), ffn2_w_up[l].astype(BF16),
                 ffn2_w_down[l].astype(BF16), mod_p, mod_s, 8, **tiles)

        outs["lfp"].append(lf_p.reshape(bp, lp, fh))
        outs["sp"].append(s_p)
        outs["lfs"].append(lf_s.reshape(bs, ls, fh))
        outs["ss"].append(s_s)

    y_prompt = _final_norm(h, ln_final, row0=0, rows=np_rows, tm=tm).reshape(bp, lp, d)
    y_sample = _final_norm(h, ln_final, row0=np_rows, rows=ns_rows, tm=tm).reshape(bs, ls, d)
    st = {k: jnp.stack(v) for k, v in outs.items()}
    kp, vp, ks, vs = kv_out
    return (y_prompt, y_sample, kp.reshape(depth, bp, lp, fh, fdh), vp.reshape(depth, bp, lp, fh, fdh),
            st["lfp"], st["sp"], ks.reshape(depth, bs, ls, fh, fdh), vs.reshape(depth, bs, ls, fh, fdh),
            st["lfs"], st["ss"])
```

```python
import functools

import jax
import jax.numpy as jnp
from jax import lax
from jax.experimental import pallas as pl
from jax.experimental.pallas import tpu as pltpu

EPS = 1e-6
MASK_VALUE = -1e30
N_MOD = 9
LANES = 128
SUBLANES = 8
V7X_VMEM_BYTES = 64 * 1024 * 1024
VMEM_LIMIT = V7X_VMEM_BYTES * 3 // 4
HGRN_SUB = 16
LOG2E = 1.4426950408889634
FFN_ROW_CHUNKS = 4
EXP2_ZERO = 152.0

F32 = jnp.float32
BF16 = jnp.bfloat16


def _params(*sem):
    return pltpu.CompilerParams(dimension_semantics=sem, vmem_limit_bytes=VMEM_LIMIT)


def _sigmoid(x):
    return 1.0 / (1.0 + jnp.exp(-x))


def _dot(a, b):
    return jnp.dot(a, b, preferred_element_type=F32)


def _dot_nt(a, b):
    return lax.dot_general(a, b, (((1,), (1,)), ((), ())), preferred_element_type=F32)


def _dot_tn(a, b):
    return lax.dot_general(a, b, (((0,), (0,)), ((), ())), preferred_element_type=F32)


def _split3(x):
    x1 = x.astype(BF16)
    r1 = x - x1.astype(F32)
    x2 = r1.astype(BF16)
    x3 = (r1 - x2.astype(F32)).astype(BF16)
    return x1, x2, x3


def _pick(is_prompt, p_ref, s_ref):
    return jnp.where(is_prompt, p_ref[...][None], s_ref[...])


def _mod_specs(k, npt, gt, n, col=None):
    if col is None:
        p = pl.BlockSpec((None, None, 1, n), lambda i, *_: (0, k, 0, 0))
        s = pl.BlockSpec((gt, None, 1, n), lambda i, *_: (jnp.maximum(i - npt, 0), k, 0, 0))
    else:
        p = pl.BlockSpec((None, None, 1, n), lambda i, j: (0, k, 0, j))
        s = pl.BlockSpec((gt, None, 1, n), lambda i, j: (jnp.maximum(i - npt, 0), k, 0, j))
    return p, s


def _ada_kernel(c_ref, w_ref, b_ref, o_ref):
    c = c_ref[...]
    a = (c * _sigmoid(c)).astype(BF16)
    o_ref[...] = _dot(a, w_ref[...].astype(BF16)) + b_ref[...]


def _ada(c_pad, w, b, layer):
    m, d = c_pad.shape
    depth, _, n = w.shape
    tn = 1024 if n % 1024 == 0 else n
    return pl.pallas_call(
        _ada_kernel,
        grid=(n // tn,),
        in_specs=[pl.BlockSpec((m, d), lambda j: (0, 0)),
                  pl.BlockSpec((None, d, tn), lambda j: (layer, 0, j)),
                  pl.BlockSpec((None, 1, tn), lambda j: (layer, 0, j))],
        out_specs=pl.BlockSpec((m, tn), lambda j: (0, j)),
        out_shape=jax.ShapeDtypeStruct((m, n), F32),
        compiler_params=_params("arbitrary"),
        name="ada",
    )(c_pad, w, b.reshape(depth, 1, n))


def _normmod_kernel(x_ref, g_ref, shp_ref, shs_ref, scp_ref, scs_ref, o_ref, *, npt, gt):
    is_p = pl.program_id(0) < npt
    x = x_ref[...]
    tm, d = x.shape
    y = x * lax.rsqrt(jnp.mean(x * x, axis=-1, keepdims=True) + EPS) * g_ref[...]
    sh = _pick(is_p, shp_ref, shs_ref)
    sc = _pick(is_p, scp_ref, scs_ref)
    y = y.reshape(gt, tm // gt, d) * (1.0 + sc) + sh
    o_ref[...] = y.reshape(tm, d).astype(o_ref.dtype)


def _normmod(x, g, mod_p, mod_s, k_shift, k_scale, *, tm, npt, gt):
    t, d = x.shape
    shp, shs = _mod_specs(k_shift, npt, gt, d)
    scp, scs = _mod_specs(k_scale, npt, gt, d)
    return pl.pallas_call(
        functools.partial(_normmod_kernel, npt=npt, gt=gt),
        grid=(t // tm,),
        in_specs=[pl.BlockSpec((tm, d), lambda i: (i, 0)),
                  pl.BlockSpec((1, d), lambda i: (0, 0)),
                  shp, shs, scp, scs],
        out_specs=pl.BlockSpec((tm, d), lambda i: (i, 0)),
        out_shape=jax.ShapeDtypeStruct((t, d), BF16),
        compiler_params=_params("parallel"),
        name="normmod",
    )(x, g.reshape(1, d), mod_p, mod_s, mod_p, mod_s)


def _rmsnorm_kernel(x_ref, g_ref, o_ref):
    x = x_ref[...]
    o_ref[...] = x * lax.rsqrt(jnp.mean(x * x, axis=-1, keepdims=True) + EPS) * g_ref[...]


def _final_norm(x, g, *, row0, rows, tm):
    d = x.shape[1]
    b0 = row0 // tm
    return pl.pallas_call(
        _rmsnorm_kernel,
        grid=(rows // tm,),
        in_specs=[pl.BlockSpec((tm, d), lambda i: (b0 + i, 0)),
                  pl.BlockSpec((1, d), lambda i: (0, 0))],
        out_specs=pl.BlockSpec((tm, d), lambda i: (i, 0)),
        out_shape=jax.ShapeDtypeStruct((rows, d), F32),
        compiler_params=_params("parallel"),
        name="final_norm",
    )(x, g.reshape(1, d))


def _ffn_kernel(n_ref, x_ref, wg_ref, wu_ref, wd_ref, gp_ref, gs_ref, o_ref, acc_ref, *, npt, gt):
    i, j = pl.program_id(0), pl.program_id(1)

    @pl.when(j == 0)
    def _():
        acc_ref[...] = jnp.zeros_like(acc_ref)

    rows = n_ref.shape[0]
    rc = rows // FFN_ROW_CHUNKS

    def gate_up(c):
        n = n_ref[c * rc:(c + 1) * rc]
        return _dot(n, wg_ref[...]), _dot(n, wu_ref[...])

    def down(c, ab):
        a, b = ab
        acc_ref[c * rc:(c + 1) * rc] += _dot((a * _sigmoid(a) * b).astype(BF16), wd_ref[...])

    ab = gate_up(0)
    for c in range(FFN_ROW_CHUNKS):
        ab_next = gate_up(c + 1) if c + 1 < FFN_ROW_CHUNKS else None
        down(c, ab)
        ab = ab_next

    @pl.when(j == pl.num_programs(1) - 1)
    def _():
        tm, d = acc_ref.shape
        gate = _pick(i < npt, gp_ref, gs_ref)
        y = (0.5 * gate) * acc_ref[...].reshape(gt, tm // gt, d)
        o_ref[...] = x_ref[...] + y.reshape(tm, d)


def _ffn(n, x, wg, wu, wd, mod_p, mod_s, k_gate, *, tm, npt, gt):
    t, d = x.shape
    f = wg.shape[1]
    tf = 512 if f % 512 == 0 else f
    gp, gs = _mod_specs(k_gate, npt, gt, d)
    return pl.pallas_call(
        functools.partial(_ffn_kernel, npt=npt, gt=gt),
        grid=(t // tm, f // tf),
        in_specs=[pl.BlockSpec((tm, d), lambda i, j: (i, 0)),
                  pl.BlockSpec((tm, d), lambda i, j: (i, 0)),
                  pl.BlockSpec((d, tf), lambda i, j: (0, j)),
                  pl.BlockSpec((d, tf), lambda i, j: (0, j)),
                  pl.BlockSpec((tf, d), lambda i, j: (j, 0)),
                  gp, gs],
        out_specs=pl.BlockSpec((tm, d), lambda i, j: (i, 0)),
        out_shape=jax.ShapeDtypeStruct((t, d), F32),
        scratch_shapes=[pltpu.VMEM((tm, d), F32)],
        compiler_params=_params("parallel", "arbitrary"),
        name="ffn",
    )(n, x, wg, wu, wd, mod_p, mod_s)


def _mm_kernel(x_ref, w_ref, *o_refs):
    r = _dot(x_ref[...], w_ref[...])
    for o_ref in o_refs:
        o_ref[...] = r.astype(o_ref.dtype)


def _mm(x, w, out_dtypes, *, tm, tn):
    t, d = x.shape
    n = w.shape[1]
    outs = pl.pallas_call(
        _mm_kernel,
        grid=(t // tm, n // tn),
        in_specs=[pl.BlockSpec((tm, d), lambda i, j: (i, 0)),
                  pl.BlockSpec((d, tn), lambda i, j: (0, j))],
        out_specs=[pl.BlockSpec((tm, tn), lambda i, j: (i, j)) for _ in out_dtypes],
        out_shape=[jax.ShapeDtypeStruct((t, n), dt) for dt in out_dtypes],
        compiler_params=_params("parallel", "arbitrary"),
        name="proj",
    )(x, w)
    return outs


def _kvproj_kernel(x_ref, w_ref, *refs, npt, heads, ct, has_prev):
    kv16_ref, kp_ref, vp_ref, ks_ref, vs_ref = refs[4:] if has_prev else refs
    is_p = pl.program_id(0) < npt
    x = x_ref[...]
    tm = x.shape[0]
    fw = heads * LANES
    for dst_p, dst_s, c0 in ((kp_ref, ks_ref, 0), (vp_ref, vs_ref, fw)):
        for cj in range(fw // ct):
            cols = slice(c0 + cj * ct, c0 + (cj + 1) * ct)
            r = _dot(x, w_ref[:, cols])
            kv16_ref[:, cols] = r.astype(kv16_ref.dtype)

            def store(dst_ref, r=r, cj=cj):
                for hh in range(ct // LANES):
                    dst_ref[pl.ds(cj * (ct // LANES) + hh, tm, stride=heads), :] = (
                        r[:, hh * LANES:(hh + 1) * LANES])

            pl.when(is_p)(functools.partial(store, dst_p))
            pl.when(jnp.logical_not(is_p))(functools.partial(store, dst_s))


def _kvproj(x, w, prev, *, layer, depth, heads, np_rows, tm):
    t, d = x.shape
    n = w.shape[1]
    npt = np_rows // tm
    ns_rows = t - np_rows
    fw = heads * LANES
    ct = 512 if fw % 512 == 0 else LANES
    blk = (None, tm * heads, LANES)
    p_spec = pl.BlockSpec(blk, lambda i: (layer, jnp.minimum(i, npt - 1), 0))
    s_spec = pl.BlockSpec(blk, lambda i: (layer, jnp.maximum(i - npt, 0), 0))
    p_shape = jax.ShapeDtypeStruct((depth, np_rows * heads, LANES), F32)
    s_shape = jax.ShapeDtypeStruct((depth, ns_rows * heads, LANES), F32)
    has_prev = prev is not None
    any_spec = pl.BlockSpec(memory_space=pl.ANY)
    return pl.pallas_call(
        functools.partial(_kvproj_kernel, npt=npt, heads=heads, ct=ct, has_prev=has_prev),
        grid=(t // tm,),
        in_specs=[pl.BlockSpec((tm, d), lambda i: (i, 0)),
                  pl.BlockSpec((d, n), lambda i: (0, 0), pipeline_mode=pl.Buffered(1))]
        + ([any_spec] * 4 if has_prev else []),
        out_specs=[pl.BlockSpec((tm, n), lambda i: (i, 0)), p_spec, p_spec, s_spec, s_spec],
        out_shape=[jax.ShapeDtypeStruct((t, n), BF16), p_shape, p_shape, s_shape, s_shape],
        input_output_aliases={2: 1, 3: 2, 4: 3, 5: 4} if has_prev else {},
        compiler_params=_params("arbitrary"),
        name="kvproj",
    )(x, w, *(prev if has_prev else ()))


def _mm_logsig_kernel(x_ref, w_ref, b_ref, o_ref):
    y = _dot(x_ref[...], w_ref[...]) + b_ref[...]
    o_ref[...] = jnp.minimum(y, 0.0) - jnp.log(1.0 + jnp.exp(-jnp.abs(y)))


def _mm_logsig(x, w, b, *, tm):
    t, d = x.shape
    n = w.shape[1]
    return pl.pallas_call(
        _mm_logsig_kernel,
        grid=(t // tm,),
        in_specs=[pl.BlockSpec((tm, d), lambda i: (i, 0)),
                  pl.BlockSpec((d, n), lambda i: (0, 0)),
                  pl.BlockSpec((1, n), lambda i: (0, 0))],
        out_specs=pl.BlockSpec((tm, n), lambda i: (i, 0)),
        out_shape=jax.ShapeDtypeStruct((t, n), F32),
        compiler_params=_params("parallel"),
        name="fox_logf",
    )(x, w, b)


def _merge_kernel(o_ref, y_ref, wa_ref, wb_ref, ga_ref, gb_ref, out_ref):
    a = _dot(o_ref[...], wa_ref[...])
    b = _dot(y_ref[...], wb_ref[...])
    out_ref[...] = (_sigmoid(ga_ref[...]) * a + _sigmoid(gb_ref[...]) * b).astype(out_ref.dtype)


def _merge(o, y, wa, wb, p32, col_ga, col_gb, *, tm, tn):
    t, hw = o.shape
    fw = y.shape[1]
    d = wa.shape[1]
    ja, jb = col_ga // tn, col_gb // tn
    return pl.pallas_call(
        _merge_kernel,
        grid=(t // tm, d // tn),
        in_specs=[pl.BlockSpec((tm, hw), lambda i, j: (i, 0)),
                  pl.BlockSpec((tm, fw), lambda i, j: (i, 0)),
                  pl.BlockSpec((hw, tn), lambda i, j: (0, j)),
                  pl.BlockSpec((fw, tn), lambda i, j: (0, j)),
                  pl.BlockSpec((tm, tn), lambda i, j: (i, ja + j)),
                  pl.BlockSpec((tm, tn), lambda i, j: (i, jb + j))],
        out_specs=pl.BlockSpec((tm, tn), lambda i, j: (i, j)),
        out_shape=jax.ShapeDtypeStruct((t, d), BF16),
        compiler_params=_params("parallel", "arbitrary"),
        name="merge",
    )(o, y, wa, wb, p32, p32)


def _outproj_kernel(m_ref, w_ref, x_ref, gp_ref, gs_ref, o_ref, *, npt, gt):
    r = _dot(m_ref[...], w_ref[...])
    tm, tn = r.shape
    gate = _pick(pl.program_id(0) < npt, gp_ref, gs_ref)
    y = gate * r.reshape(gt, tm // gt, tn)
    o_ref[...] = x_ref[...] + y.reshape(tm, tn)


def _outproj(m, w, x, mod_p, mod_s, k_gate, *, tm, tn, npt, gt):
    t, d = x.shape
    gp, gs = _mod_specs(k_gate, npt, gt, tn, col=True)
    return pl.pallas_call(
        functools.partial(_outproj_kernel, npt=npt, gt=gt),
        grid=(t // tm, d // tn),
        in_specs=[pl.BlockSpec((tm, d), lambda i, j: (i, 0)),
                  pl.BlockSpec((d, tn), lambda i, j: (0, j)),
                  pl.BlockSpec((tm, tn), lambda i, j: (i, j)),
                  gp, gs],
        out_specs=pl.BlockSpec((tm, tn), lambda i, j: (i, j)),
        out_shape=jax.ShapeDtypeStruct((t, d), F32),
        compiler_params=_params("parallel", "arbitrary"),
        name="outproj",
    )(m, w, x, mod_p, mod_s)


def _per_head(fn, hb):
    return jnp.concatenate([fn(slice(h * LANES, (h + 1) * LANES)) for h in range(hb)], axis=1)


def _hgrn_heads(q16, z, v16, ag, sts, lg, hgn, *, layer):
    C, W = z.shape
    hb = W // LANES
    sub = min(HGRN_SUB, C)

    def head_sum(x):
        return _per_head(lambda ln: jnp.broadcast_to(
            jnp.sum(x[:, ln], axis=-1, keepdims=True), (x.shape[0], LANES)), hb)

    e = jnp.exp(lg - jnp.max(lg, axis=0, keepdims=True))
    soft = e / jnp.sum(e, axis=0, keepdims=True)
    lb = jnp.zeros((1, W), F32)
    for i in range(1, layer + 1):
        lb = lb + soft[i:i + 1]

    ez = jnp.exp(-jnp.abs(z))
    r = 1.0 / (1.0 + ez)
    sig_pos = jnp.where(z >= 0, r, ez * r)
    sig_neg = jnp.where(z >= 0, ez * r, r)
    g = jnp.log(lb + (1.0 - lb) * sig_pos) * LOG2E
    kin = (1.0 - lb) * sig_neg

    row = lax.broadcasted_iota(jnp.int32, (C, C), 0)
    col = lax.broadcasted_iota(jnp.int32, (C, C), 1)
    tri = (row >= col).astype(BF16)
    g1, g2, g3 = _split3(g)
    cum = _dot(tri, g1) + _dot(tri, g2) + _dot(tri, g3)
    cum_k = cum - jnp.log(kin) * LOG2E

    q = q16.astype(F32)
    v = v16.astype(F32)
    qe_all = (q * jnp.exp2(cum)).astype(BF16)
    sts16 = [st.astype(BF16) for st in sts]
    nsub = C // sub
    o_inter = _per_head(lambda ln: _dot_nt(qe_all[:, ln], sts16[ln.start // LANES]), hb)
    att = {}
    for i in range(1, nsub):
        r0 = i * sub
        b_i = cum[r0 - 1:r0]
        qe = (q[r0:r0 + sub] * jnp.exp2(cum[r0:r0 + sub] - b_i)).astype(BF16)
        ke = (kin[:r0] * jnp.exp2(b_i - cum[:r0])).astype(BF16)
        for h in range(hb):
            ln = slice(h * LANES, (h + 1) * LANES)
            att[i, h] = _dot_nt(qe[:, ln], ke[:, ln]).astype(BF16)
    o_off = {i: _per_head(lambda ln: _dot(att[i, ln.start // LANES], v16[:i * sub, ln]), hb)
             for i in range(1, nsub)}

    row8 = lax.broadcasted_iota(jnp.int32, (SUBLANES, 1), 0)
    outs = []
    for i in range(nsub):
        r0 = i * sub
        acc = o_inter[r0:r0 + sub]
        if i > 0:
            acc = acc + o_off[i]
        for u in range(sub // SUBLANES):
            t0 = r0 + u * SUBLANES
            cum_t, q_t = cum[t0:t0 + SUBLANES], q[t0:t0 + SUBLANES]
            acc_t = acc[u * SUBLANES:(u + 1) * SUBLANES]
            for s in range(r0, t0 + SUBLANES):
                d = cum_t - cum_k[s:s + 1]
                if s >= t0:
                    d = jnp.where(row8 >= s - t0, d, MASK_VALUE)
                acc_t = acc_t + head_sum(q_t * jnp.exp2(d)) * v[s:s + 1]
            outs.append(acc_t)
    o = jnp.concatenate(outs, axis=0)

    o = o * lax.rsqrt(head_sum(o * o) * (1.0 / LANES) + EPS) * hgn
    agf = ag.astype(F32)
    o = o * (agf * _sigmoid(agf))

    last = cum[C - 1:C]
    kdec = (kin * jnp.exp2(last - cum)).astype(BF16)
    decay = jnp.exp2(last)
    new_sts = [sts[h] * decay[:, h * LANES:(h + 1) * LANES]
               + _dot_tn(v16[:, h * LANES:(h + 1) * LANES], kdec[:, h * LANES:(h + 1) * LANES])
               for h in range(hb)]
    return o, new_sts


def _hgrn_kernel(q_ref, z_ref, v_ref, ag_ref, s0_ref, lbl_ref, hgn_ref, o_ref, sout_ref, st_ref,
                 *, layer, nc, hb):
    c = pl.program_id(2)

    @pl.when(c == 0)
    def _():
        for hh in range(hb):
            st_ref[hh] = s0_ref[hh].T

    o, new_sts = _hgrn_heads(q_ref[...], z_ref[...], v_ref[...], ag_ref[...],
                             [st_ref[hh] for hh in range(hb)], lbl_ref[...], hgn_ref[...], layer=layer)
    o_ref[...] = o.astype(o_ref.dtype)
    for hh in range(hb):
        st_ref[hh] = new_sts[hh]

    @pl.when(c == nc - 1)
    def _():
        for hh in range(hb):
            sout_ref[hh] = st_ref[hh].T


def _hgrn(p16, p32, s0, lb_logits, hg_norm, *, layer, row0, nb, length, chunk, hb,
          col_q, col_v, col_z, col_ag):
    heads = s0.shape[1]
    nc = length // chunk
    rb0 = row0 // chunk
    depth = lb_logits.shape[0]
    w = hb * LANES

    def rows(b, h, c):
        return rb0 + b * nc + c

    return pl.pallas_call(
        functools.partial(_hgrn_kernel, layer=layer, nc=nc, hb=hb),
        grid=(nb, heads // hb, nc),
        in_specs=[pl.BlockSpec((chunk, w), lambda b, h, c: (rows(b, h, c), col_q // hb + h)),
                  pl.BlockSpec((chunk, w), lambda b, h, c: (rows(b, h, c), col_z // hb + h)),
                  pl.BlockSpec((chunk, w), lambda b, h, c: (rows(b, h, c), col_v // hb + h)),
                  pl.BlockSpec((chunk, w), lambda b, h, c: (rows(b, h, c), col_ag // hb + h)),
                  pl.BlockSpec((None, hb, LANES, LANES), lambda b, h, c: (b, h, 0, 0)),
                  pl.BlockSpec((depth, w), lambda b, h, c: (0, h)),
                  pl.BlockSpec((1, w), lambda b, h, c: (0, h))],
        out_specs=[pl.BlockSpec((chunk, w), lambda b, h, c: (b * nc + c, h)),
                   pl.BlockSpec((None, hb, LANES, LANES), lambda b, h, c: (b, h, 0, 0))],
        out_shape=[jax.ShapeDtypeStruct((nb * length, heads * LANES), BF16),
                   jax.ShapeDtypeStruct((nb, heads, LANES, LANES), F32)],
        scratch_shapes=[pltpu.VMEM((hb, LANES, LANES), F32)],
        compiler_params=_params("parallel", "parallel", "arbitrary"),
        name="hgrn",
    )(p16, p32, p16, p32, s0, lb_logits, hg_norm)


def _cumsum_kernel(x_ref, o_ref, carry_ref):
    @pl.when(pl.program_id(0) == 0)
    def _():
        carry_ref[...] = jnp.zeros_like(carry_ref)

    x = x_ref[...]
    bw = x.shape[1]
    row = lax.broadcasted_iota(jnp.int32, (bw, bw), 0)
    col = lax.broadcasted_iota(jnp.int32, (bw, bw), 1)
    upper = (row <= col).astype(BF16)
    x1, x2, x3 = _split3(x)
    c = _dot(x1, upper) + _dot(x2, upper) + _dot(x3, upper) + carry_ref[...]
    o_ref[...] = c
    carry_ref[...] = c[:, bw - 1:bw]


def _cumsum_lanes(x):
    r, n = x.shape
    bw = 256 if n % 256 == 0 else LANES
    return pl.pallas_call(
        _cumsum_kernel,
        grid=(n // bw,),
        in_specs=[pl.BlockSpec((r, bw), lambda j: (0, j))],
        out_specs=pl.BlockSpec((r, bw), lambda j: (0, j)),
        out_shape=jax.ShapeDtypeStruct((r, n), F32),
        scratch_shapes=[pltpu.VMEM((r, 1), F32)],
        compiler_params=_params("arbitrary"),
        name="cumsum",
    )(x)


def _fox_update(s, v, m_ref, l_ref, acc_ref, slot, masked):
    tq, tk = s.shape
    if masked:
        row = lax.broadcasted_iota(jnp.int32, (tq, tk), 0)
        col = lax.broadcasted_iota(jnp.int32, (tq, tk), 1)
        s = jnp.where(row >= col, s, MASK_VALUE)
    m_old = m_ref[slot]
    m_new = jnp.maximum(m_old, jnp.max(s, axis=-1, keepdims=True))
    alpha = jnp.exp2(m_old - m_new)
    if tk % LANES == 0:
        ps = [jnp.exp2(s[:, c * LANES:(c + 1) * LANES] - m_new) for c in range(tk // LANES)]
        p_lanes = functools.reduce(jnp.add, ps)
        p = jnp.concatenate(ps, axis=1)
    else:
        p = jnp.exp2(s - m_new[:, :tk])
        p_lanes = jnp.concatenate([p, jnp.zeros((tq, LANES - tk), F32)], axis=1)
    l_ref[slot] = alpha * l_ref[slot] + p_lanes
    acc_ref[slot] = alpha * acc_ref[slot] + _dot(p.astype(BF16), v)
    m_ref[slot] = m_new


def _fox_init(m_ref, l_ref, acc_ref):
    m_ref[...] = jnp.full_like(m_ref, MASK_VALUE)
    l_ref[...] = jnp.zeros_like(l_ref)
    acc_ref[...] = jnp.zeros_like(acc_ref)


def _fox_out(l_ref, acc_ref, slot):
    return acc_ref[slot] / jnp.sum(l_ref[slot], axis=-1, keepdims=True)


def _fox_tile(q, k, v, bias, m_ref, l_ref, acc_ref, slot, masked):
    _fox_update(_dot_nt(q, k) + bias, v, m_ref, l_ref, acc_ref, slot, masked)


def _fox_prompt_kernel(q_ref, k_ref, v_ref, c_ref, o_ref, s_ref, m_ref, l_ref, acc_ref, kn_ref, *, tb):
    qi = pl.program_id(1)
    nk = k_ref.shape[0] // tb

    @pl.when(qi == 0)
    def _():
        def kn_body(i, mx):
            kb = k_ref[pl.ds(pl.multiple_of(i * tb, tb), tb), :].astype(F32)
            return jnp.maximum(mx, jnp.max(jnp.sum(kb * kb, axis=-1, keepdims=True), axis=0, keepdims=True))
        kn_ref[...] = jnp.broadcast_to(lax.fori_loop(0, nk, kn_body, jnp.zeros((1, 1), F32)), kn_ref.shape)

    _fox_init(m_ref, l_ref, acc_ref)
    base = pl.multiple_of(qi * (2 * tb), 2 * tb)
    c_first = c_ref[:, pl.ds(base, LANES)][:, 0:1]
    qf = q_ref[...].astype(F32)
    qn2 = jnp.max(jnp.sum(qf * qf, axis=-1, keepdims=True), axis=0, keepdims=True)
    qk_max = jnp.sqrt(qn2 * kn_ref[:, 0:1])

    def scores(half, slot, kstart):
        bias = (c_first - c_ref[:, pl.ds(kstart, tb)]) * LOG2E
        s_ref[half, slot] = _dot_nt(q_ref[half * tb:(half + 1) * tb], k_ref[pl.ds(kstart, tb), :]) + bias

    def update(half, slot, kstart, masked):
        _fox_update(s_ref[half, slot], v_ref[pl.ds(kstart, tb), :], m_ref, l_ref, acc_ref, half, masked)

    def tile_start(i):
        return pl.multiple_of(jnp.maximum(i, 0) * tb, tb)

    def live(kstart):
        c_last = c_ref[:, pl.ds(kstart + (tb - LANES), LANES)][:, LANES - 1:LANES]
        s_max = qk_max * (1.0 + 2.0 ** -6) + (c_first - c_last) * LOG2E
        m_min = jnp.min(jnp.min(m_ref[...], axis=0), axis=0, keepdims=True)[:, 0:1]
        return jnp.where(s_max > m_min - EXP2_ZERO, 1, 0)[0, 0]

    top = pl.multiple_of(base + tb, tb)
    scores(1, 1, top)
    scores(0, 0, base)
    scores(1, 0, base)
    update(1, 1, top, True)
    update(0, 0, base, True)
    update(1, 0, base, False)

    n = 2 * qi
    scores(0, 0, tile_start(n - 1))
    scores(1, 0, tile_start(n - 1))

    def body(carry):
        j, _ = carry
        k0, k1, k2 = tile_start(n - 1 - 2 * j), tile_start(n - 2 - 2 * j), tile_start(n - 3 - 2 * j)
        for half in range(2):
            scores(half, 1, k1)
            update(half, 0, k0, False)
        for half in range(2):
            scores(half, 0, k2)
            update(half, 1, k1, False)
        return j + 1, live(k2)

    lax.while_loop(lambda carry: jnp.logical_and(carry[0] < qi, carry[1] > 0), body,
                   (jnp.int32(0), live(tile_start(n - 1))))
    for half in range(2):
        o_ref[half * tb:(half + 1) * tb] = _fox_out(l_ref, acc_ref, half).astype(o_ref.dtype)


def _fox_prompt(p16, kv16, c, *, heads, length, col_q, col_k, col_v, tb):
    return pl.pallas_call(
        functools.partial(_fox_prompt_kernel, tb=tb),
        grid=(heads, length // (2 * tb)),
        in_specs=[pl.BlockSpec((2 * tb, LANES), lambda h, qi: (qi, col_q + h)),
                  pl.BlockSpec((length, LANES), lambda h, qi: (0, col_k + h)),
                  pl.BlockSpec((length, LANES), lambda h, qi: (0, col_v + h)),
                  pl.BlockSpec((None, 1, length), lambda h, qi: (h, 0, 0))],
        out_specs=pl.BlockSpec((2 * tb, LANES), lambda h, qi: (qi, h)),
        out_shape=jax.ShapeDtypeStruct((length, heads * LANES), BF16),
        scratch_shapes=[pltpu.VMEM((2, 2, tb, tb), F32), pltpu.VMEM((2, tb, LANES), F32),
                        pltpu.VMEM((2, tb, LANES), F32), pltpu.VMEM((2, tb, LANES), F32),
                        pltpu.VMEM((1, LANES), F32)],
        compiler_params=_params("parallel", "arbitrary"),
        name="fox_prompt",
    )(p16, kv16, kv16, c)


def _fox_sample_kernel(q_ref, kn_ref, vn_ref, kc_ref, vc_ref, cq_ref, ckc_ref, ckn_ref, o_ref,
                       m_ref, l_ref, acc_ref, *, heads, nck):
    c = pl.program_id(1)
    tkc = kc_ref.shape[0] // heads

    @pl.when(c == 0)
    def _():
        _fox_init(m_ref, l_ref, acc_ref)

    cq_all = cq_ref[...] * LOG2E

    def head_lanes(h):
        return slice(h * LANES, (h + 1) * LANES)

    for h in range(heads):
        k = kc_ref[pl.ds(h, tkc, stride=heads), :].astype(BF16)
        v = vc_ref[pl.ds(h, tkc, stride=heads), :].astype(BF16)
        bias = cq_all[:, h:h + 1] - ckc_ref[h] * LOG2E
        _fox_tile(q_ref[:, head_lanes(h)], k, v, bias, m_ref, l_ref, acc_ref, h, False)

    @pl.when(c == nck - 1)
    def _():
        for h in range(heads):
            bias = cq_all[:, h:h + 1] - ckn_ref[h] * LOG2E
            _fox_tile(q_ref[:, head_lanes(h)], kn_ref[:, head_lanes(h)], vn_ref[:, head_lanes(h)],
                      bias, m_ref, l_ref, acc_ref, h, True)
            o_ref[:, head_lanes(h)] = _fox_out(l_ref, acc_ref, h).astype(o_ref.dtype)


def _fox_sample(p16, kv16, cache_k, cache_v, cq, ckc, ckn, *, layer, row0, nb, ls, heads,
                col_q, col_k, col_v, tkc):
    past = cache_k.shape[2] // heads
    nck = past // tkc
    rb0 = row0 // ls
    w = heads * LANES
    stat = pltpu.VMEM((heads, ls, LANES), F32)
    return pl.pallas_call(
        functools.partial(_fox_sample_kernel, heads=heads, nck=nck),
        grid=(nb, nck),
        in_specs=[pl.BlockSpec((ls, w), lambda b, c: (rb0 + b, col_q // heads)),
                  pl.BlockSpec((ls, w), lambda b, c: (rb0 + b, col_k // heads)),
                  pl.BlockSpec((ls, w), lambda b, c: (rb0 + b, col_v // heads)),
                  pl.BlockSpec((None, None, tkc * heads, LANES), lambda b, c: (layer, b, c, 0)),
                  pl.BlockSpec((None, None, tkc * heads, LANES), lambda b, c: (layer, b, c, 0)),
                  pl.BlockSpec((None, ls, heads), lambda b, c: (b, 0, 0)),
                  pl.BlockSpec((None, heads, 1, tkc), lambda b, c: (b, 0, 0, c)),
                  pl.BlockSpec((None, heads, 1, ls), lambda b, c: (b, 0, 0, 0))],
        out_specs=pl.BlockSpec((ls, w), lambda b, c: (b, 0)),
        out_shape=jax.ShapeDtypeStruct((nb * ls, w), BF16),
        scratch_shapes=[stat, stat, stat],
        compiler_params=_params("parallel", "arbitrary"),
        name="fox_sample",
    )(p16, kv16, kv16, cache_k, cache_v, cq, ckc, ckn)


def _row_tile(np_rows, ns_rows, ls):
    for tm in (512, 256, 128, 64, 32):
        if np_rows % tm == 0 and ns_rows % tm == 0 and tm % ls == 0:
            return tm
    raise ValueError("no row tile fits the prompt / sample row counts")


def kernel(x_prompt, x_sample, c_prompt, c_sample, cache_fox_k, cache_fox_v, cache_fox_logf,
           state_hgrn, ada_w, ada_b, ln_ffn1, ln_mix, ln_ffn2, ffn1_w_gate, ffn1_w_up,
           ffn1_w_down, ffn2_w_gate, ffn2_w_up, ffn2_w_down, w_in, hgrn_lb_logits,
           hgrn_norm_g, fox_f_bias, w_branch_a, w_branch_b, w_out, ln_final):
    bp, lp, d = x_prompt.shape
    bs, ls, _ = x_sample.shape
    depth = ada_w.shape[0]
    _, _, past, fh, fdh = cache_fox_k.shape
    _, _, hh, dk, dv = state_hgrn.shape
    assert bp == 1 and dk == LANES and dv == LANES and fdh == LANES
    assert ls % SUBLANES == 0 and d % LANES == 0
    np_rows, ns_rows = bp * lp, bs * ls
    t = np_rows + ns_rows
    hw, fw = hh * LANES, fh * LANES
    tm = _row_tile(np_rows, ns_rows, ls)
    npt, gt = np_rows // tm, tm // ls
    tn = 512 if (d % 512 == 0 and hw % 512 == 0 and fw % 512 == 0) else LANES
    chunk_p = 64 if lp % 64 == 0 else lp
    tb = 512 if lp % 1024 == 0 else lp // 2
    tkc = 512 if past % 512 == 0 else past
    hb = next(n for n in (8, 4, 2, 1) if hh % n == 0)
    tm_kv = min(tm, 256)
    tm_big = 2 * tm if (np_rows % (2 * tm) == 0 and ns_rows % (2 * tm) == 0) else tm
    tiles = dict(tm=tm, npt=npt, gt=gt)
    tiles_big = dict(tm=tm_big, npt=np_rows // tm_big, gt=tm_big // ls)
    q_scale = LOG2E * float(fdh) ** -0.5

    x = jnp.concatenate([x_prompt.reshape(np_rows, d), x_sample.reshape(ns_rows, d)], axis=0)
    c_all = jnp.concatenate([c_prompt, c_sample], axis=0)
    m_pad = -(-c_all.shape[0] // SUBLANES) * SUBLANES
    c_pad = jnp.pad(c_all, ((0, m_pad - c_all.shape[0]), (0, 0)))
    s0_prompt = jnp.zeros((bp, hh, dk, dv), F32)

    sizes = (hw, hw, hw, hw, fw, fw, fw, fh, d, d)
    offs = [0]
    for s in sizes:
        offs.append(offs[-1] + s)
    o_aq, o_af, o_ai, o_ag, o_fq, o_fk, o_fv, o_ffl, o_ga, o_gb = offs[:10]

    outs = {k: [] for k in ("lfp", "sp", "lfs", "ss")}
    cache_k = cache_fox_k.reshape(depth, bs, past * fh, fdh)
    cache_v = cache_fox_v.reshape(depth, bs, past * fh, fdh)
    kv_out = None
    h = x
    for l in range(depth):
        wi = w_in[l]
        w16 = jnp.concatenate([wi[:, o_aq:o_aq + hw], wi[:, o_ai:o_ai + hw],
                               wi[:, o_fq:o_fq + fw] * q_scale], axis=1).astype(BF16)
        w32 = jnp.concatenate([wi[:, o_af:o_af + hw], wi[:, o_ag:o_ag + hw],
                               wi[:, o_ga:o_ga + d], wi[:, o_gb:o_gb + d]], axis=1).astype(BF16)
        wkv = wi[:, o_fk:o_fk + 2 * fw].astype(BF16)
        wffl = jnp.pad(wi[:, o_ffl:o_ffl + fh], ((0, 0), (0, LANES - fh))).astype(BF16)
        bffl = jnp.pad(fox_f_bias[l], (0, LANES - fh)).reshape(1, LANES)

        mod = _ada(c_pad, ada_w, ada_b, l)
        mod_p = mod[0:bp].reshape(bp, N_MOD, 1, d)
        mod_s = mod[bp:bp + bs].reshape(bs, N_MOD, 1, d)

        n1 = _normmod(h, ln_ffn1[l], mod_p, mod_s, 0, 1, **tiles)
        h = _ffn(n1, h, ffn1_w_gate[l].astype(BF16), ffn1_w_up[l].astype(BF16),
                 ffn1_w_down[l].astype(BF16), mod_p, mod_s, 2, **tiles)

        n2 = _normmod(h, ln_mix[l], mod_p, mod_s, 3, 4, **tiles)
        (p16,) = _mm(n2, w16, (BF16,), tm=tm_big, tn=tn)
        (p32,) = _mm(n2, w32, (F32,), tm=tm_big, tn=tn)
        kv16, *kv_out = _kvproj(n2, wkv, kv_out, layer=l, depth=depth, heads=fh,
                                np_rows=np_rows, tm=tm_kv)
        lf_pad = _mm_logsig(n2, wffl, bffl, tm=tm)

        cb = lambda off: off // LANES
        hg_cols = dict(col_q=0, col_v=cb(hw), col_z=0, col_ag=cb(hw), hb=hb)
        o_p, s_p = _hgrn(p16, p32, s0_prompt, hgrn_lb_logits, hgrn_norm_g[l].reshape(1, hw),
                         layer=l, row0=0, nb=bp, length=lp, chunk=chunk_p, **hg_cols)
        o_s, s_s = _hgrn(p16, p32, state_hgrn[l], hgrn_lb_logits, hgrn_norm_g[l].reshape(1, hw),
                         layer=l, row0=np_rows, nb=bs, length=ls, chunk=ls, **hg_cols)

        lf = lf_pad[:, :fh]
        lf_p, lf_s = lf[:np_rows], lf[np_rows:]
        c_p = _cumsum_lanes(lf_p.T)
        cache_t = cache_fox_logf[l].transpose(0, 2, 1).reshape(bs * fh, past)
        new_t = lf_s.reshape(bs, ls, fh).transpose(0, 2, 1).reshape(bs * fh, ls)
        n_all = -(-(past + ls) // LANES) * LANES
        c_s = _cumsum_lanes(jnp.concatenate(
            [cache_t, new_t, jnp.zeros((bs * fh, n_all - past - ls), F32)], axis=1))
        cq_s = c_s[:, past:past + ls].reshape(bs, fh, ls).transpose(0, 2, 1)
        ckc_s = c_s[:, :past].reshape(bs, fh, 1, past)
        ckn_s = c_s[:, past:past + ls].reshape(bs, fh, 1, ls)

        fox_cols = dict(col_q=cb(2 * hw), col_k=0, col_v=cb(fw))
        y_p = _fox_prompt(p16, kv16, c_p.reshape(fh, 1, lp), heads=fh, length=lp, tb=tb, **fox_cols)
        y_s = _fox_sample(p16, kv16, cache_k, cache_v, cq_s, ckc_s, ckn_s, layer=l,
                          row0=np_rows, nb=bs, ls=ls, heads=fh, tkc=tkc, **fox_cols)

        o_all = jnp.concatenate([o_p, o_s], axis=0)
        y_all = jnp.concatenate([y_p, y_s], axis=0)
        merged = _merge(o_all, y_all, w_branch_a[l].astype(BF16), w_branch_b[l].astype(BF16),
                        p32, 2 * hw, 2 * hw + d, tm=tm_big, tn=tn)
        h = _outproj(merged, w_out[l].astype(BF16), h, mod_p, mod_s, 5, tn=tn, **tiles_big)

        n3 = _normmod(h, ln_ffn2[l], mod_p, mod_s, 6, 7, **tiles)
        h = _ffn(n3, h, ffn2_w_gate[l].astype(BF16), ffn2_w_up[l].astype(BF16),
                 ffn2_w_down[l].astype(BF16), mod_p, mod_s, 8, **tiles)

        outs["lfp"].append(lf_p.reshape(bp, lp, fh))
        outs["sp"].append(s_p)
        outs["lfs"].append(lf_s.reshape(bs, ls, fh))
        outs["ss"].append(s_s)

    y_prompt = _final_norm(h, ln_final, row0=0, rows=np_rows, tm=tm).reshape(bp, lp, d)
    y_sample = _final_norm(h, ln_final, row0=np_rows, rows=ns_rows, tm=tm).reshape(bs, ls, d)
    st = {k: jnp.stack(v) for k, v in outs.items()}
    kp, vp, ks, vs = kv_out
    return (y_prompt, y_sample, kp.reshape(depth, bp, lp, fh, fdh), vp.reshape(depth, bp, lp, fh, fdh),
            st["lfp"], st["sp"], ks.reshape(depth, bs, ls, fh, fdh), vs.reshape(depth, bs, ls, fh, fdh),
            st["lfs"], st["ss"])
```

```python
import functools

import jax
import jax.numpy as jnp
from jax import lax
from jax.experimental import pallas as pl
from jax.experimental.pallas import tpu as pltpu

EPS = 1e-6
MASK_VALUE = -1e30
N_MOD = 9
LANES = 128
SUBLANES = 8
V7X_VMEM_BYTES = 64 * 1024 * 1024
VMEM_LIMIT = V7X_VMEM_BYTES * 3 // 4
HGRN_SUB = 16
LOG2E = 1.4426950408889634
EXP2_ZERO = 152.0

F32 = jnp.float32
BF16 = jnp.bfloat16


def _params(*sem):
    return pltpu.CompilerParams(dimension_semantics=sem, vmem_limit_bytes=VMEM_LIMIT)


def _sigmoid(x):
    return 1.0 / (1.0 + jnp.exp(-x))


def _dot(a, b):
    return jnp.dot(a, b, preferred_element_type=F32)


def _dot_nt(a, b):
    return lax.dot_general(a, b, (((1,), (1,)), ((), ())), preferred_element_type=F32)


def _dot_tn(a, b):
    return lax.dot_general(a, b, (((0,), (0,)), ((), ())), preferred_element_type=F32)


def _split3(x):
    x1 = x.astype(BF16)
    r1 = x - x1.astype(F32)
    x2 = r1.astype(BF16)
    x3 = (r1 - x2.astype(F32)).astype(BF16)
    return x1, x2, x3


def _pick(is_prompt, p_ref, s_ref):
    return jnp.where(is_prompt, p_ref[...][None], s_ref[...])


def _mod_specs(k, npt, gt, n, col=None):
    if col is None:
        p = pl.BlockSpec((None, None, 1, n), lambda i, *_: (0, k, 0, 0))
        s = pl.BlockSpec((gt, None, 1, n), lambda i, *_: (jnp.maximum(i - npt, 0), k, 0, 0))
    else:
        p = pl.BlockSpec((None, None, 1, n), lambda i, j: (0, k, 0, j))
        s = pl.BlockSpec((gt, None, 1, n), lambda i, j: (jnp.maximum(i - npt, 0), k, 0, j))
    return p, s


def _ada_kernel(c_ref, w_ref, b_ref, o_ref):
    c = c_ref[...]
    a = (c * _sigmoid(c)).astype(BF16)
    o_ref[...] = _dot(a, w_ref[...].astype(BF16)) + b_ref[...]


def _ada(c_pad, w, b, layer):
    m, d = c_pad.shape
    depth, _, n = w.shape
    tn = 1024 if n % 1024 == 0 else n
    return pl.pallas_call(
        _ada_kernel,
        grid=(n // tn,),
        in_specs=[pl.BlockSpec((m, d), lambda j: (0, 0)),
                  pl.BlockSpec((None, d, tn), lambda j: (layer, 0, j)),
                  pl.BlockSpec((None, 1, tn), lambda j: (layer, 0, j))],
        out_specs=pl.BlockSpec((m, tn), lambda j: (0, j)),
        out_shape=jax.ShapeDtypeStruct((m, n), F32),
        compiler_params=_params("arbitrary"),
        name="ada",
    )(c_pad, w, b.reshape(depth, 1, n))


def _normmod_kernel(x_ref, g_ref, shp_ref, shs_ref, scp_ref, scs_ref, o_ref, *, npt, gt):
    is_p = pl.program_id(0) < npt
    x = x_ref[...]
    tm, d = x.shape
    y = x * lax.rsqrt(jnp.mean(x * x, axis=-1, keepdims=True) + EPS) * g_ref[...]
    sh = _pick(is_p, shp_ref, shs_ref)
    sc = _pick(is_p, scp_ref, scs_ref)
    y = y.reshape(gt, tm // gt, d) * (1.0 + sc) + sh
    o_ref[...] = y.reshape(tm, d).astype(o_ref.dtype)


def _normmod(x, g, mod_p, mod_s, k_shift, k_scale, *, tm, npt, gt):
    t, d = x.shape
    shp, shs = _mod_specs(k_shift, npt, gt, d)
    scp, scs = _mod_specs(k_scale, npt, gt, d)
    return pl.pallas_call(
        functools.partial(_normmod_kernel, npt=npt, gt=gt),
        grid=(t // tm,),
        in_specs=[pl.BlockSpec((tm, d), lambda i: (i, 0)),
                  pl.BlockSpec((1, d), lambda i: (0, 0)),
                  shp, shs, scp, scs],
        out_specs=pl.BlockSpec((tm, d), lambda i: (i, 0)),
        out_shape=jax.ShapeDtypeStruct((t, d), BF16),
        compiler_params=_params("parallel"),
        name="normmod",
    )(x, g.reshape(1, d), mod_p, mod_s, mod_p, mod_s)


def _rmsnorm_kernel(x_ref, g_ref, o_ref):
    x = x_ref[...]
    o_ref[...] = x * lax.rsqrt(jnp.mean(x * x, axis=-1, keepdims=True) + EPS) * g_ref[...]


def _final_norm(x, g, *, row0, rows, tm):
    d = x.shape[1]
    b0 = row0 // tm
    return pl.pallas_call(
        _rmsnorm_kernel,
        grid=(rows // tm,),
        in_specs=[pl.BlockSpec((tm, d), lambda i: (b0 + i, 0)),
                  pl.BlockSpec((1, d), lambda i: (0, 0))],
        out_specs=pl.BlockSpec((tm, d), lambda i: (i, 0)),
        out_shape=jax.ShapeDtypeStruct((rows, d), F32),
        compiler_params=_params("parallel"),
        name="final_norm",
    )(x, g.reshape(1, d))


def _ffn_kernel(n_ref, x_ref, wg_ref, wu_ref, wd_ref, gp_ref, gs_ref, o_ref, acc_ref, *, npt, gt):
    i, j = pl.program_id(0), pl.program_id(1)

    @pl.when(j == 0)
    def _():
        acc_ref[...] = jnp.zeros_like(acc_ref)

    n = n_ref[...]
    a = _dot(n, wg_ref[...])
    b = _dot(n, wu_ref[...])
    acc_ref[...] += _dot((a * _sigmoid(a) * b).astype(BF16), wd_ref[...])

    @pl.when(j == pl.num_programs(1) - 1)
    def _():
        tm, d = acc_ref.shape
        gate = _pick(i < npt, gp_ref, gs_ref)
        y = (0.5 * gate) * acc_ref[...].reshape(gt, tm // gt, d)
        o_ref[...] = x_ref[...] + y.reshape(tm, d)


def _ffn(n, x, wg, wu, wd, layer, mod_p, mod_s, k_gate, *, tm, npt, gt):
    t, d = x.shape
    f = wg.shape[2]
    tf = 512 if f % 512 == 0 else f
    gp, gs = _mod_specs(k_gate, npt, gt, d)
    return pl.pallas_call(
        functools.partial(_ffn_kernel, npt=npt, gt=gt),
        grid=(t // tm, f // tf),
        in_specs=[pl.BlockSpec((tm, d), lambda i, j: (i, 0)),
                  pl.BlockSpec((tm, d), lambda i, j: (i, 0)),
                  pl.BlockSpec((None, d, tf), lambda i, j: (layer, 0, j)),
                  pl.BlockSpec((None, d, tf), lambda i, j: (layer, 0, j)),
                  pl.BlockSpec((None, tf, d), lambda i, j: (layer, j, 0)),
                  gp, gs],
        out_specs=pl.BlockSpec((tm, d), lambda i, j: (i, 0)),
        out_shape=jax.ShapeDtypeStruct((t, d), F32),
        scratch_shapes=[pltpu.VMEM((tm, d), F32)],
        compiler_params=_params("parallel", "arbitrary"),
        name="ffn",
    )(n, x, wg, wu, wd, mod_p, mod_s)


def _mm_kernel(x_ref, w_ref, *o_refs):
    r = _dot(x_ref[...], w_ref[...])
    for o_ref in o_refs:
        o_ref[...] = r.astype(o_ref.dtype)


def _mm(x, w, out_dtypes, *, tm, tn):
    t, d = x.shape
    n = w.shape[1]
    outs = pl.pallas_call(
        _mm_kernel,
        grid=(t // tm, n // tn),
        in_specs=[pl.BlockSpec((tm, d), lambda i, j: (i, 0)),
                  pl.BlockSpec((d, tn), lambda i, j: (0, j))],
        out_specs=[pl.BlockSpec((tm, tn), lambda i, j: (i, j)) for _ in out_dtypes],
        out_shape=[jax.ShapeDtypeStruct((t, n), dt) for dt in out_dtypes],
        compiler_params=_params("parallel", "arbitrary"),
        name="proj",
    )(x, w)
    return outs


def _kvproj_kernel(x_ref, w_ref, *refs, npt, heads, ct, has_prev):
    kv16_ref, kp_ref, vp_ref, ks_ref, vs_ref = refs[4:] if has_prev else refs
    is_p = pl.program_id(0) < npt
    tm = x_ref.shape[0]
    fw = heads * LANES

    def project(k_dst, v_dst):
        x = x_ref[...]
        for dst_ref, c0 in ((k_dst, 0), (v_dst, fw)):
            for cj in range(fw // ct):
                cols = slice(c0 + cj * ct, c0 + (cj + 1) * ct)
                r = _dot(x, w_ref[:, cols])
                kv16_ref[:, cols] = r.astype(kv16_ref.dtype)
                for hh in range(ct // LANES):
                    dst_ref[pl.ds(cj * (ct // LANES) + hh, tm, stride=heads), :] = (
                        r[:, hh * LANES:(hh + 1) * LANES])

    pl.when(is_p)(functools.partial(project, kp_ref, vp_ref))
    pl.when(jnp.logical_not(is_p))(functools.partial(project, ks_ref, vs_ref))


def _kvproj(x, w, prev, *, layer, depth, heads, np_rows, tm):
    t, d = x.shape
    n = w.shape[1]
    npt = np_rows // tm
    ns_rows = t - np_rows
    fw = heads * LANES
    ct = 512 if fw % 512 == 0 else LANES
    blk = (None, tm * heads, LANES)
    p_spec = pl.BlockSpec(blk, lambda i: (layer, jnp.minimum(i, npt - 1), 0))
    s_spec = pl.BlockSpec(blk, lambda i: (layer, jnp.maximum(i - npt, 0), 0))
    p_shape = jax.ShapeDtypeStruct((depth, np_rows * heads, LANES), F32)
    s_shape = jax.ShapeDtypeStruct((depth, ns_rows * heads, LANES), F32)
    has_prev = prev is not None
    any_spec = pl.BlockSpec(memory_space=pl.ANY)
    return pl.pallas_call(
        functools.partial(_kvproj_kernel, npt=npt, heads=heads, ct=ct, has_prev=has_prev),
        grid=(t // tm,),
        in_specs=[pl.BlockSpec((tm, d), lambda i: (i, 0)),
                  pl.BlockSpec((d, n), lambda i: (0, 0), pipeline_mode=pl.Buffered(1))]
        + ([any_spec] * 4 if has_prev else []),
        out_specs=[pl.BlockSpec((tm, n), lambda i: (i, 0)), p_spec, p_spec, s_spec, s_spec],
        out_shape=[jax.ShapeDtypeStruct((t, n), BF16), p_shape, p_shape, s_shape, s_shape],
        input_output_aliases={2: 1, 3: 2, 4: 3, 5: 4} if has_prev else {},
        compiler_params=_params("arbitrary"),
        name="kvproj",
    )(x, w, *(prev if has_prev else ()))


def _mm_logsig_kernel(x_ref, w_ref, b_ref, o_ref):
    y = _dot(x_ref[...], w_ref[...]) + b_ref[...]
    o_ref[...] = jnp.minimum(y, 0.0) - jnp.log(1.0 + jnp.exp(-jnp.abs(y)))


def _mm_logsig(x, w, b, *, tm):
    t, d = x.shape
    n = w.shape[1]
    return pl.pallas_call(
        _mm_logsig_kernel,
        grid=(t // tm,),
        in_specs=[pl.BlockSpec((tm, d), lambda i: (i, 0)),
                  pl.BlockSpec((d, n), lambda i: (0, 0)),
                  pl.BlockSpec((1, n), lambda i: (0, 0))],
        out_specs=pl.BlockSpec((tm, n), lambda i: (i, 0)),
        out_shape=jax.ShapeDtypeStruct((t, n), F32),
        compiler_params=_params("parallel"),
        name="fox_logf",
    )(x, w, b)


def _merge_kernel(o_ref, y_ref, wa_ref, wb_ref, ga_ref, gb_ref, out_ref):
    a = _dot(o_ref[...], wa_ref[...])
    b = _dot(y_ref[...], wb_ref[...])
    out_ref[...] = (_sigmoid(ga_ref[...]) * a + _sigmoid(gb_ref[...]) * b).astype(out_ref.dtype)


def _merge(o, y, wa, wb, layer, p32, col_ga, col_gb, *, tm, tn):
    t, hw = o.shape
    fw = y.shape[1]
    d = wa.shape[2]
    ja, jb = col_ga // tn, col_gb // tn
    return pl.pallas_call(
        _merge_kernel,
        grid=(t // tm, d // tn),
        in_specs=[pl.BlockSpec((tm, hw), lambda i, j: (i, 0)),
                  pl.BlockSpec((tm, fw), lambda i, j: (i, 0)),
                  pl.BlockSpec((None, hw, tn), lambda i, j: (layer, 0, j)),
                  pl.BlockSpec((None, fw, tn), lambda i, j: (layer, 0, j)),
                  pl.BlockSpec((tm, tn), lambda i, j: (i, ja + j)),
                  pl.BlockSpec((tm, tn), lambda i, j: (i, jb + j))],
        out_specs=pl.BlockSpec((tm, tn), lambda i, j: (i, j)),
        out_shape=jax.ShapeDtypeStruct((t, d), BF16),
        compiler_params=_params("parallel", "arbitrary"),
        name="merge",
    )(o, y, wa, wb, p32, p32)


def _outproj_kernel(m_ref, w_ref, x_ref, gp_ref, gs_ref, o_ref, *, npt, gt):
    r = _dot(m_ref[...], w_ref[...])
    tm, tn = r.shape
    gate = _pick(pl.program_id(0) < npt, gp_ref, gs_ref)
    y = gate * r.reshape(gt, tm // gt, tn)
    o_ref[...] = x_ref[...] + y.reshape(tm, tn)


def _outproj(m, w, layer, x, mod_p, mod_s, k_gate, *, tm, tn, npt, gt):
    t, d = x.shape
    gp, gs = _mod_specs(k_gate, npt, gt, tn, col=True)
    return pl.pallas_call(
        functools.partial(_outproj_kernel, npt=npt, gt=gt),
        grid=(t // tm, d // tn),
        in_specs=[pl.BlockSpec((tm, d), lambda i, j: (i, 0)),
                  pl.BlockSpec((None, d, tn), lambda i, j: (layer, 0, j)),
                  pl.BlockSpec((tm, tn), lambda i, j: (i, j)),
                  gp, gs],
        out_specs=pl.BlockSpec((tm, tn), lambda i, j: (i, j)),
        out_shape=jax.ShapeDtypeStruct((t, d), F32),
        compiler_params=_params("parallel", "arbitrary"),
        name="outproj",
    )(m, w, x, mod_p, mod_s)


def _per_head(fn, hb):
    return jnp.concatenate([fn(slice(h * LANES, (h + 1) * LANES)) for h in range(hb)], axis=1)


def _hgrn_heads(q16, z, v16, ag, sts, lg, hgn, *, layer):
    C, W = z.shape
    hb = W // LANES
    sub = min(HGRN_SUB, C)

    def head_sum(x):
        return _per_head(lambda ln: jnp.broadcast_to(
            jnp.sum(x[:, ln], axis=-1, keepdims=True), (x.shape[0], LANES)), hb)

    e = jnp.exp(lg - jnp.max(lg, axis=0, keepdims=True))
    soft = e / jnp.sum(e, axis=0, keepdims=True)
    lb = jnp.zeros((1, W), F32)
    for i in range(1, layer + 1):
        lb = lb + soft[i:i + 1]

    ez = jnp.exp(-jnp.abs(z))
    r = 1.0 / (1.0 + ez)
    sig_pos = jnp.where(z >= 0, r, ez * r)
    sig_neg = jnp.where(z >= 0, ez * r, r)
    g = jnp.log(lb + (1.0 - lb) * sig_pos) * LOG2E
    kin = (1.0 - lb) * sig_neg

    row = lax.broadcasted_iota(jnp.int32, (C, C), 0)
    col = lax.broadcasted_iota(jnp.int32, (C, C), 1)
    tri = (row >= col).astype(BF16)
    g1, g2, g3 = _split3(g)
    cum = _dot(tri, g1) + _dot(tri, g2) + _dot(tri, g3)
    cum_k = cum - jnp.log(kin) * LOG2E

    q = q16.astype(F32)
    v = v16.astype(F32)
    qe_all = (q * jnp.exp2(cum)).astype(BF16)
    sts16 = [st.astype(BF16) for st in sts]
    nsub = C // sub
    o_inter = _per_head(lambda ln: _dot_nt(qe_all[:, ln], sts16[ln.start // LANES]), hb)
    att = {}
    for i in range(1, nsub):
        r0 = i * sub
        b_i = cum[r0 - 1:r0]
        qe = (q[r0:r0 + sub] * jnp.exp2(cum[r0:r0 + sub] - b_i)).astype(BF16)
        ke = (kin[:r0] * jnp.exp2(b_i - cum[:r0])).astype(BF16)
        for h in range(hb):
            ln = slice(h * LANES, (h + 1) * LANES)
            att[i, h] = _dot_nt(qe[:, ln], ke[:, ln]).astype(BF16)
    o_off = {i: _per_head(lambda ln: _dot(att[i, ln.start // LANES], v16[:i * sub, ln]), hb)
             for i in range(1, nsub)}

    row8 = lax.broadcasted_iota(jnp.int32, (SUBLANES, 1), 0)
    outs = []
    for i in range(nsub):
        r0 = i * sub
        acc = o_inter[r0:r0 + sub]
        if i > 0:
            acc = acc + o_off[i]
        for u in range(sub // SUBLANES):
            t0 = r0 + u * SUBLANES
            cum_t, q_t = cum[t0:t0 + SUBLANES], q[t0:t0 + SUBLANES]
            acc_t = acc[u * SUBLANES:(u + 1) * SUBLANES]
            for s in range(r0, t0 + SUBLANES):
                d = cum_t - cum_k[s:s + 1]
                if s >= t0:
                    d = jnp.where(row8 >= s - t0, d, MASK_VALUE)
                acc_t = acc_t + head_sum(q_t * jnp.exp2(d)) * v[s:s + 1]
            outs.append(acc_t)
    o = jnp.concatenate(outs, axis=0)

    o = o * lax.rsqrt(head_sum(o * o) * (1.0 / LANES) + EPS) * hgn
    agf = ag.astype(F32)
    o = o * (agf * _sigmoid(agf))

    last = cum[C - 1:C]
    kdec = (kin * jnp.exp2(last - cum)).astype(BF16)
    decay = jnp.exp2(last)
    new_sts = [sts[h] * decay[:, h * LANES:(h + 1) * LANES]
               + _dot_tn(v16[:, h * LANES:(h + 1) * LANES], kdec[:, h * LANES:(h + 1) * LANES])
               for h in range(hb)]
    return o, new_sts


def _hgrn_kernel(q_ref, z_ref, v_ref, ag_ref, s0_ref, lbl_ref, hgn_ref, *refs, layer, nc, hb, has_prev):
    o_ref, sout_ref, st_ref = refs[1:] if has_prev else refs
    c = pl.program_id(2)

    @pl.when(c == 0)
    def _():
        for hh in range(hb):
            st_ref[hh] = s0_ref[hh].T

    o, new_sts = _hgrn_heads(q_ref[...], z_ref[...], v_ref[...], ag_ref[...],
                             [st_ref[hh] for hh in range(hb)], lbl_ref[...], hgn_ref[...], layer=layer)
    o_ref[...] = o.astype(o_ref.dtype)
    for hh in range(hb):
        st_ref[hh] = new_sts[hh]

    @pl.when(c == nc - 1)
    def _():
        for hh in range(hb):
            sout_ref[hh] = st_ref[hh].T


def _hgrn(p16, p32, s0, lb_logits, hg_norm, prev, *, layer, row0, nb, length, chunk, hb,
          col_q, col_v, col_z, col_ag):
    heads = s0.shape[1]
    nc = length // chunk
    rb0 = row0 // chunk
    depth = lb_logits.shape[0]
    w = hb * LANES
    has_prev = prev is not None

    def rows(b, h, c):
        return rb0 + b * nc + c

    return pl.pallas_call(
        functools.partial(_hgrn_kernel, layer=layer, nc=nc, hb=hb, has_prev=has_prev),
        grid=(nb, heads // hb, nc),
        in_specs=[pl.BlockSpec((chunk, w), lambda b, h, c: (rows(b, h, c), col_q // hb + h)),
                  pl.BlockSpec((chunk, w), lambda b, h, c: (rows(b, h, c), col_z // hb + h)),
                  pl.BlockSpec((chunk, w), lambda b, h, c: (rows(b, h, c), col_v // hb + h)),
                  pl.BlockSpec((chunk, w), lambda b, h, c: (rows(b, h, c), col_ag // hb + h)),
                  pl.BlockSpec((None, hb, LANES, LANES), lambda b, h, c: (b, h, 0, 0)),
                  pl.BlockSpec((depth, w), lambda b, h, c: (0, h)),
                  pl.BlockSpec((1, w), lambda b, h, c: (0, h))]
        + ([pl.BlockSpec(memory_space=pl.ANY)] if has_prev else []),
        out_specs=[pl.BlockSpec((chunk, w), lambda b, h, c: (rows(b, h, c), h)),
                   pl.BlockSpec((None, hb, LANES, LANES), lambda b, h, c: (b, h, 0, 0))],
        out_shape=[jax.ShapeDtypeStruct((p16.shape[0], heads * LANES), BF16),
                   jax.ShapeDtypeStruct((nb, heads, LANES, LANES), F32)],
        input_output_aliases={7: 0} if has_prev else {},
        scratch_shapes=[pltpu.VMEM((hb, LANES, LANES), F32)],
        compiler_params=_params("parallel", "parallel", "arbitrary"),
        name="hgrn",
    )(p16, p32, p16, p32, s0, lb_logits, hg_norm, *([prev] if has_prev else []))


def _cumsum_kernel(x_ref, o_ref, carry_ref):
    @pl.when(pl.program_id(0) == 0)
    def _():
        carry_ref[...] = jnp.zeros_like(carry_ref)

    x = x_ref[...]
    bw = x.shape[1]
    row = lax.broadcasted_iota(jnp.int32, (bw, bw), 0)
    col = lax.broadcasted_iota(jnp.int32, (bw, bw), 1)
    upper = (row <= col).astype(BF16)
    x1, x2, x3 = _split3(x)
    c = _dot(x1, upper) + _dot(x2, upper) + _dot(x3, upper) + carry_ref[...]
    o_ref[...] = c
    carry_ref[...] = c[:, bw - 1:bw]


def _cumsum_lanes(x):
    r, n = x.shape
    bw = 256 if n % 256 == 0 else LANES
    return pl.pallas_call(
        _cumsum_kernel,
        grid=(n // bw,),
        in_specs=[pl.BlockSpec((r, bw), lambda j: (0, j))],
        out_specs=pl.BlockSpec((r, bw), lambda j: (0, j)),
        out_shape=jax.ShapeDtypeStruct((r, n), F32),
        scratch_shapes=[pltpu.VMEM((r, 1), F32)],
        compiler_params=_params("arbitrary"),
        name="cumsum",
    )(x)


def _fox_update(s, v, m_ref, l_ref, acc_ref, slot, masked):
    tq, tk = s.shape
    if masked:
        row = lax.broadcasted_iota(jnp.int32, (tq, tk), 0)
        col = lax.broadcasted_iota(jnp.int32, (tq, tk), 1)
        s = jnp.where(row >= col, s, MASK_VALUE)
    m_old = m_ref[slot]
    m_new = jnp.maximum(m_old, jnp.max(s, axis=-1, keepdims=True))
    alpha = jnp.exp2(m_old - m_new)
    if tk % LANES == 0:
        ps = [jnp.exp2(s[:, c * LANES:(c + 1) * LANES] - m_new) for c in range(tk // LANES)]
        p_lanes = functools.reduce(jnp.add, ps)
        p = jnp.concatenate(ps, axis=1)
    else:
        p = jnp.exp2(s - m_new[:, :tk])
        p_lanes = jnp.concatenate([p, jnp.zeros((tq, LANES - tk), F32)], axis=1)
    l_ref[slot] = alpha * l_ref[slot] + p_lanes
    acc_ref[slot] = alpha * acc_ref[slot] + _dot(p.astype(BF16), v)
    m_ref[slot] = m_new


def _fox_init(m_ref, l_ref, acc_ref):
    m_ref[...] = jnp.full_like(m_ref, MASK_VALUE)
    l_ref[...] = jnp.zeros_like(l_ref)
    acc_ref[...] = jnp.zeros_like(acc_ref)


def _fox_out(l_ref, acc_ref, slot):
    return acc_ref[slot] / jnp.sum(l_ref[slot], axis=-1, keepdims=True)


def _fox_tile(q, k, v, bias, m_ref, l_ref, acc_ref, slot, masked):
    _fox_update(_dot_nt(q, k) + bias, v, m_ref, l_ref, acc_ref, slot, masked)


def _fox_prompt_kernel(q_ref, k_ref, v_ref, c_ref, o_ref, s_ref, m_ref, l_ref, acc_ref, kn_ref, *, tb):
    qi = pl.program_id(1)
    nk = k_ref.shape[0] // tb

    @pl.when(qi == 0)
    def _():
        def kn_body(i, mx):
            kb = k_ref[pl.ds(pl.multiple_of(i * tb, tb), tb), :].astype(F32)
            return jnp.maximum(mx, jnp.max(jnp.sum(kb * kb, axis=-1, keepdims=True), axis=0, keepdims=True))
        kn_ref[...] = jnp.broadcast_to(lax.fori_loop(0, nk, kn_body, jnp.zeros((1, 1), F32)), kn_ref.shape)

    _fox_init(m_ref, l_ref, acc_ref)
    base = pl.multiple_of(qi * (2 * tb), 2 * tb)
    c_first = c_ref[:, pl.ds(base, LANES)][:, 0:1]
    qf = q_ref[...].astype(F32)
    qn2 = jnp.max(jnp.sum(qf * qf, axis=-1, keepdims=True), axis=0, keepdims=True)
    qk_max = jnp.sqrt(qn2 * kn_ref[:, 0:1])

    def scores(half, slot, kstart):
        bias = (c_first - c_ref[:, pl.ds(kstart, tb)]) * LOG2E
        s_ref[half, slot] = _dot_nt(q_ref[half * tb:(half + 1) * tb], k_ref[pl.ds(kstart, tb), :]) + bias

    def update(half, slot, kstart, masked):
        _fox_update(s_ref[half, slot], v_ref[pl.ds(kstart, tb), :], m_ref, l_ref, acc_ref, half, masked)

    def tile_start(i):
        return pl.multiple_of(jnp.maximum(i, 0) * tb, tb)

    def live(kstart):
        c_last = c_ref[:, pl.ds(kstart + (tb - LANES), LANES)][:, LANES - 1:LANES]
        s_max = qk_max * (1.0 + 2.0 ** -6) + (c_first - c_last) * LOG2E
        m_min = jnp.min(jnp.min(m_ref[...], axis=0), axis=0, keepdims=True)[:, 0:1]
        return jnp.where(s_max > m_min - EXP2_ZERO, 1, 0)[0, 0]

    top = pl.multiple_of(base + tb, tb)
    scores(1, 1, top)
    scores(0, 0, base)
    scores(1, 0, base)
    update(1, 1, top, True)
    update(0, 0, base, True)
    update(1, 0, base, False)

    n = 2 * qi
    scores(0, 0, tile_start(n - 1))
    scores(1, 0, tile_start(n - 1))

    def body(carry):
        j, _ = carry
        k0, k1, k2 = tile_start(n - 1 - 2 * j), tile_start(n - 2 - 2 * j), tile_start(n - 3 - 2 * j)
        for half in range(2):
            scores(half, 1, k1)
            update(half, 0, k0, False)
        for half in range(2):
            scores(half, 0, k2)
            update(half, 1, k1, False)
        return j + 1, live(k2)

    lax.while_loop(lambda carry: jnp.logical_and(carry[0] < qi, carry[1] > 0), body,
                   (jnp.int32(0), live(tile_start(n - 1))))
    for half in range(2):
        o_ref[half * tb:(half + 1) * tb] = _fox_out(l_ref, acc_ref, half).astype(o_ref.dtype)


def _fox_prompt(p16, kv16, c, *, heads, length, col_q, col_k, col_v, tb):
    return pl.pallas_call(
        functools.partial(_fox_prompt_kernel, tb=tb),
        grid=(heads, length // (2 * tb)),
        in_specs=[pl.BlockSpec((2 * tb, LANES), lambda h, qi: (qi, col_q + h)),
                  pl.BlockSpec((length, LANES), lambda h, qi: (0, col_k + h)),
                  pl.BlockSpec((length, LANES), lambda h, qi: (0, col_v + h)),
                  pl.BlockSpec((None, 1, length), lambda h, qi: (h, 0, 0))],
        out_specs=pl.BlockSpec((2 * tb, LANES), lambda h, qi: (qi, h)),
        out_shape=jax.ShapeDtypeStruct((p16.shape[0], heads * LANES), BF16),
        scratch_shapes=[pltpu.VMEM((2, 2, tb, tb), F32), pltpu.VMEM((2, tb, LANES), F32),
                        pltpu.VMEM((2, tb, LANES), F32), pltpu.VMEM((2, tb, LANES), F32),
                        pltpu.VMEM((1, LANES), F32)],
        compiler_params=_params("parallel", "arbitrary"),
        name="fox_prompt",
    )(p16, kv16, kv16, c)


def _fox_sample_kernel(q_ref, kn_ref, vn_ref, kc_ref, vc_ref, cq_ref, ckc_ref, ckn_ref, ckl_ref, _, o_ref,
                       m_ref, l_ref, acc_ref, q1_ref, *, heads, nck):
    c = pl.program_id(1)
    tkc = kc_ref.shape[0] // heads
    cq_all = cq_ref[...] * LOG2E

    def head_lanes(h):
        return slice(h * LANES, (h + 1) * LANES)

    @pl.when(c == 0)
    def _():
        _fox_init(m_ref, l_ref, acc_ref)
        for h in range(heads):
            q = q_ref[:, head_lanes(h)]
            bias = cq_all[:, h:h + 1] - ckn_ref[h] * LOG2E
            _fox_tile(q, kn_ref[:, head_lanes(h)], vn_ref[:, head_lanes(h)], bias, m_ref, l_ref, acc_ref, h, True)
            q1 = jnp.max(jnp.sum(jnp.abs(q.astype(F32)), axis=-1, keepdims=True), axis=0, keepdims=True)
            q1_ref[h] = jnp.broadcast_to(q1, (1, LANES))

    k_max = jnp.max(jnp.max(jnp.abs(kc_ref[...]), axis=0, keepdims=True), axis=-1, keepdims=True)
    cq_max = jnp.max(cq_all, axis=0, keepdims=True)
    c_last = ckl_ref[...] * LOG2E
    live = jnp.zeros((1, 1), jnp.int32)
    for h in range(heads):
        s_max = q1_ref[h][:, 0:1] * k_max * (1.0 + 2.0 ** -6) + (cq_max[:, h:h + 1] - c_last[:, h:h + 1])
        m_min = jnp.min(m_ref[h], axis=0, keepdims=True)[:, 0:1]
        live = jnp.maximum(live, jnp.where(s_max > m_min - EXP2_ZERO, 1, 0))

    @pl.when(live[0, 0] > 0)
    def _():
        for h in range(heads):
            k = kc_ref[pl.ds(h, tkc, stride=heads), :].astype(BF16)
            v = vc_ref[pl.ds(h, tkc, stride=heads), :].astype(BF16)
            bias = cq_all[:, h:h + 1] - ckc_ref[h] * LOG2E
            _fox_tile(q_ref[:, head_lanes(h)], k, v, bias, m_ref, l_ref, acc_ref, h, False)

    @pl.when(c == nck - 1)
    def _():
        for h in range(heads):
            o_ref[:, head_lanes(h)] = _fox_out(l_ref, acc_ref, h).astype(o_ref.dtype)


def _fox_sample(p16, kv16, cache_k, cache_v, cq, ckc, ckn, prev, *, layer, row0, nb, ls, heads,
                col_q, col_k, col_v, tkc):
    past = cache_k.shape[2] // heads
    nck = past // tkc
    rb0 = row0 // ls
    w = heads * LANES
    stat = pltpu.VMEM((heads, ls, LANES), F32)
    ckl = ckc[:, :, 0, tkc - 1::tkc].transpose(0, 2, 1)[:, :, None, :]

    def chunk(c):
        return nck - 1 - c

    return pl.pallas_call(
        functools.partial(_fox_sample_kernel, heads=heads, nck=nck),
        grid=(nb, nck),
        in_specs=[pl.BlockSpec((ls, w), lambda b, c: (rb0 + b, col_q // heads)),
                  pl.BlockSpec((ls, w), lambda b, c: (rb0 + b, col_k // heads)),
                  pl.BlockSpec((ls, w), lambda b, c: (rb0 + b, col_v // heads)),
                  pl.BlockSpec((None, None, tkc * heads, LANES), lambda b, c: (layer, b, chunk(c), 0)),
                  pl.BlockSpec((None, None, tkc * heads, LANES), lambda b, c: (layer, b, chunk(c), 0)),
                  pl.BlockSpec((None, ls, heads), lambda b, c: (b, 0, 0)),
                  pl.BlockSpec((None, heads, 1, tkc), lambda b, c: (b, 0, 0, chunk(c))),
                  pl.BlockSpec((None, heads, 1, ls), lambda b, c: (b, 0, 0, 0)),
                  pl.BlockSpec((None, None, 1, heads), lambda b, c: (b, chunk(c), 0, 0)),
                  pl.BlockSpec(memory_space=pl.ANY)],
        out_specs=pl.BlockSpec((ls, w), lambda b, c: (rb0 + b, 0)),
        out_shape=jax.ShapeDtypeStruct(prev.shape, BF16),
        input_output_aliases={9: 0},
        scratch_shapes=[stat, stat, stat, pltpu.VMEM((heads, 1, LANES), F32)],
        compiler_params=_params("parallel", "arbitrary"),
        name="fox_sample",
    )(p16, kv16, kv16, cache_k, cache_v, cq, ckc, ckn, ckl, prev)


def _row_tile(np_rows, ns_rows, ls):
    for tm in (512, 256, 128, 64, 32):
        if np_rows % tm == 0 and ns_rows % tm == 0 and tm % ls == 0:
            return tm
    raise ValueError("no row tile fits the prompt / sample row counts")


def kernel(x_prompt, x_sample, c_prompt, c_sample, cache_fox_k, cache_fox_v, cache_fox_logf,
           state_hgrn, ada_w, ada_b, ln_ffn1, ln_mix, ln_ffn2, ffn1_w_gate, ffn1_w_up,
           ffn1_w_down, ffn2_w_gate, ffn2_w_up, ffn2_w_down, w_in, hgrn_lb_logits,
           hgrn_norm_g, fox_f_bias, w_branch_a, w_branch_b, w_out, ln_final):
    bp, lp, d = x_prompt.shape
    bs, ls, _ = x_sample.shape
    depth = ada_w.shape[0]
    _, _, past, fh, fdh = cache_fox_k.shape
    _, _, hh, dk, dv = state_hgrn.shape
    assert bp == 1 and dk == LANES and dv == LANES and fdh == LANES
    assert ls % SUBLANES == 0 and d % LANES == 0
    np_rows, ns_rows = bp * lp, bs * ls
    t = np_rows + ns_rows
    hw, fw = hh * LANES, fh * LANES
    tm = _row_tile(np_rows, ns_rows, ls)
    npt, gt = np_rows // tm, tm // ls
    tn = 512 if (d % 512 == 0 and hw % 512 == 0 and fw % 512 == 0) else LANES
    chunk_p = 64 if lp % 64 == 0 else lp
    tb = 256 if lp % 512 == 0 else lp // 2
    tkc = 512 if past % 512 == 0 else past
    hb = next(n for n in (8, 4, 2, 1) if hh % n == 0)
    tm_kv = min(tm, 256)
    tm_big = 2 * tm if (np_rows % (2 * tm) == 0 and ns_rows % (2 * tm) == 0) else tm
    tiles = dict(tm=tm, npt=npt, gt=gt)
    tiles_big = dict(tm=tm_big, npt=np_rows // tm_big, gt=tm_big // ls)
    q_scale = LOG2E * float(fdh) ** -0.5

    x = jnp.concatenate([x_prompt.reshape(np_rows, d), x_sample.reshape(ns_rows, d)], axis=0)
    c_all = jnp.concatenate([c_prompt, c_sample], axis=0)
    m_pad = -(-c_all.shape[0] // SUBLANES) * SUBLANES
    c_pad = jnp.pad(c_all, ((0, m_pad - c_all.shape[0]), (0, 0)))
    s0_prompt = jnp.zeros((bp, hh, dk, dv), F32)

    sizes = (hw, hw, hw, hw, fw, fw, fw, fh, d, d)
    offs = [0]
    for s in sizes:
        offs.append(offs[-1] + s)
    o_aq, o_af, o_ai, o_ag, o_fq, o_fk, o_fv, o_ffl, o_ga, o_gb = offs[:10]

    outs = {k: [] for k in ("lfp", "sp", "lfs", "ss")}
    cache_k = cache_fox_k.reshape(depth, bs, past * fh, fdh)
    cache_v = cache_fox_v.reshape(depth, bs, past * fh, fdh)
    kv_out = None
    f1g, f1u, f1d = ffn1_w_gate.astype(BF16), ffn1_w_up.astype(BF16), ffn1_w_down.astype(BF16)
    f2g, f2u, f2d = ffn2_w_gate.astype(BF16), ffn2_w_up.astype(BF16), ffn2_w_down.astype(BF16)
    wba, wbb, wo = w_branch_a.astype(BF16), w_branch_b.astype(BF16), w_out.astype(BF16)
    h = x
    for l in range(depth):
        wi = w_in[l]
        w16 = jnp.concatenate([wi[:, o_aq:o_aq + hw], wi[:, o_ai:o_ai + hw],
                               wi[:, o_fq:o_fq + fw] * q_scale], axis=1).astype(BF16)
        w32 = jnp.concatenate([wi[:, o_af:o_af + hw], wi[:, o_ag:o_ag + hw],
                               wi[:, o_ga:o_ga + d], wi[:, o_gb:o_gb + d]], axis=1).astype(BF16)
        wkv = wi[:, o_fk:o_fk + 2 * fw].astype(BF16)
        wffl = jnp.pad(wi[:, o_ffl:o_ffl + fh], ((0, 0), (0, LANES - fh))).astype(BF16)
        bffl = jnp.pad(fox_f_bias[l], (0, LANES - fh)).reshape(1, LANES)

        mod = _ada(c_pad, ada_w, ada_b, l)
        mod_p = mod[0:bp].reshape(bp, N_MOD, 1, d)
        mod_s = mod[bp:bp + bs].reshape(bs, N_MOD, 1, d)

        n1 = _normmod(h, ln_ffn1[l], mod_p, mod_s, 0, 1, **tiles)
        h = _ffn(n1, h, f1g, f1u, f1d, l, mod_p, mod_s, 2, **tiles)

        n2 = _normmod(h, ln_mix[l], mod_p, mod_s, 3, 4, **tiles)
        (p16,) = _mm(n2, w16, (BF16,), tm=tm_big, tn=tn)
        (p32,) = _mm(n2, w32, (F32,), tm=tm_big, tn=tn)
        kv16, *kv_out = _kvproj(n2, wkv, kv_out, layer=l, depth=depth, heads=fh,
                                np_rows=np_rows, tm=tm_kv)
        lf_pad = _mm_logsig(n2, wffl, bffl, tm=tm)

        cb = lambda off: off // LANES
        hg_cols = dict(col_q=0, col_v=cb(hw), col_z=0, col_ag=cb(hw), hb=hb)
        o_all, s_p = _hgrn(p16, p32, s0_prompt, hgrn_lb_logits, hgrn_norm_g[l].reshape(1, hw), None,
                           layer=l, row0=0, nb=bp, length=lp, chunk=chunk_p, **hg_cols)
        o_all, s_s = _hgrn(p16, p32, state_hgrn[l], hgrn_lb_logits, hgrn_norm_g[l].reshape(1, hw), o_all,
                           layer=l, row0=np_rows, nb=bs, length=ls, chunk=ls, **hg_cols)

        lf = lf_pad[:, :fh]
        lf_p, lf_s = lf[:np_rows], lf[np_rows:]
        c_p = _cumsum_lanes(lf_p.T)
        cache_t = cache_fox_logf[l].transpose(0, 2, 1).reshape(bs * fh, past)
        new_t = lf_s.reshape(bs, ls, fh).transpose(0, 2, 1).reshape(bs * fh, ls)
        n_all = -(-(past + ls) // LANES) * LANES
        c_s = _cumsum_lanes(jnp.concatenate(
            [cache_t, new_t, jnp.zeros((bs * fh, n_all - past - ls), F32)], axis=1))
        cq_s = c_s[:, past:past + ls].reshape(bs, fh, ls).transpose(0, 2, 1)
        ckc_s = c_s[:, :past].reshape(bs, fh, 1, past)
        ckn_s = c_s[:, past:past + ls].reshape(bs, fh, 1, ls)

        fox_cols = dict(col_q=cb(2 * hw), col_k=0, col_v=cb(fw))
        y_all = _fox_prompt(p16, kv16, c_p.reshape(fh, 1, lp), heads=fh, length=lp, tb=tb, **fox_cols)
        y_all = _fox_sample(p16, kv16, cache_k, cache_v, cq_s, ckc_s, ckn_s, y_all, layer=l,
                            row0=np_rows, nb=bs, ls=ls, heads=fh, tkc=tkc, **fox_cols)

        merged = _merge(o_all, y_all, wba, wbb, l, p32, 2 * hw, 2 * hw + d, tm=tm_big, tn=tn)
        h = _outproj(merged, wo, l, h, mod_p, mod_s, 5, tn=tn, **tiles_big)

        n3 = _normmod(h, ln_ffn2[l], mod_p, mod_s, 6, 7, **tiles)
        h = _ffn(n3, h, f2g, f2u, f2d, l, mod_p, mod_s, 8, **tiles)

        outs["lfp"].append(lf_p.reshape(bp, lp, fh))
        outs["sp"].append(s_p)
        outs["lfs"].append(lf_s.reshape(bs, ls, fh))
        outs["ss"].append(s_s)

    y_prompt = _final_norm(h, ln_final, row0=0, rows=np_rows, tm=tm).reshape(bp, lp, d)
    y_sample = _final_norm(h, ln_final, row0=np_rows, rows=ns_rows, tm=tm).reshape(bs, ls, d)
    st = {k: jnp.stack(v) for k, v in outs.items()}
    kp, vp, ks, vs = kv_out
    return (y_prompt, y_sample, kp.reshape(depth, bp, lp, fh, fdh), vp.reshape(depth, bp, lp, fh, fdh),
            st["lfp"], st["sp"], ks.reshape(depth, bs, ls, fh, fdh), vs.reshape(depth, bs, ls, fh, fdh),
            st["lfs"], st["ss"])
```

```python
import functools

import jax
import jax.numpy as jnp
from jax import lax
from jax.experimental import pallas as pl
from jax.experimental.pallas import tpu as pltpu

EPS = 1e-6
MASK_VALUE = -1e30
N_MOD = 9
LANES = 128
SUBLANES = 8
V7X_VMEM_BYTES = 64 * 1024 * 1024
VMEM_LIMIT = V7X_VMEM_BYTES * 3 // 4
HGRN_SUB = 16
LOG2E = 1.4426950408889634
EXP2_ZERO = 152.0

F32 = jnp.float32
BF16 = jnp.bfloat16


def _params(*sem):
    return pltpu.CompilerParams(dimension_semantics=sem, vmem_limit_bytes=VMEM_LIMIT)


def _sigmoid(x):
    return 1.0 / (1.0 + jnp.exp(-x))


def _dot(a, b):
    return jnp.dot(a, b, preferred_element_type=F32)


def _dot_nt(a, b):
    return lax.dot_general(a, b, (((1,), (1,)), ((), ())), preferred_element_type=F32)


def _dot_tn(a, b):
    return lax.dot_general(a, b, (((0,), (0,)), ((), ())), preferred_element_type=F32)


def _split3(x):
    x1 = x.astype(BF16)
    r1 = x - x1.astype(F32)
    x2 = r1.astype(BF16)
    x3 = (r1 - x2.astype(F32)).astype(BF16)
    return x1, x2, x3


def _pick(is_prompt, p_ref, s_ref):
    return jnp.where(is_prompt, p_ref[...][None], s_ref[...])


def _mod_specs(k, npt, gt, n, col=None):
    if col is None:
        p = pl.BlockSpec((None, None, 1, n), lambda i, *_: (0, k, 0, 0))
        s = pl.BlockSpec((gt, None, 1, n), lambda i, *_: (jnp.maximum(i - npt, 0), k, 0, 0))
    else:
        p = pl.BlockSpec((None, None, 1, n), lambda i, j: (0, k, 0, j))
        s = pl.BlockSpec((gt, None, 1, n), lambda i, j: (jnp.maximum(i - npt, 0), k, 0, j))
    return p, s


def _ada_kernel(c_ref, w_ref, b_ref, o_ref):
    c = c_ref[...]
    a = (c * _sigmoid(c)).astype(BF16)
    o_ref[...] = _dot(a, w_ref[...].astype(BF16)) + b_ref[...]


def _ada(c_pad, w, b, layer):
    m, d = c_pad.shape
    depth, _, n = w.shape
    tn = 1024 if n % 1024 == 0 else n
    return pl.pallas_call(
        _ada_kernel,
        grid=(n // tn,),
        in_specs=[pl.BlockSpec((m, d), lambda j: (0, 0)),
                  pl.BlockSpec((None, d, tn), lambda j: (layer, 0, j)),
                  pl.BlockSpec((None, 1, tn), lambda j: (layer, 0, j))],
        out_specs=pl.BlockSpec((m, tn), lambda j: (0, j)),
        out_shape=jax.ShapeDtypeStruct((m, n), F32),
        compiler_params=_params("arbitrary"),
        name="ada",
    )(c_pad, w, b.reshape(depth, 1, n))


def _normmod_kernel(x_ref, g_ref, shp_ref, shs_ref, scp_ref, scs_ref, o_ref, *, npt, gt):
    is_p = pl.program_id(0) < npt
    x = x_ref[...]
    tm, d = x.shape
    y = x * lax.rsqrt(jnp.mean(x * x, axis=-1, keepdims=True) + EPS) * g_ref[...]
    sh = _pick(is_p, shp_ref, shs_ref)
    sc = _pick(is_p, scp_ref, scs_ref)
    y = y.reshape(gt, tm // gt, d) * (1.0 + sc) + sh
    o_ref[...] = y.reshape(tm, d).astype(o_ref.dtype)


def _normmod(x, g, mod_p, mod_s, k_shift, k_scale, *, tm, npt, gt):
    t, d = x.shape
    shp, shs = _mod_specs(k_shift, npt, gt, d)
    scp, scs = _mod_specs(k_scale, npt, gt, d)
    return pl.pallas_call(
        functools.partial(_normmod_kernel, npt=npt, gt=gt),
        grid=(t // tm,),
        in_specs=[pl.BlockSpec((tm, d), lambda i: (i, 0)),
                  pl.BlockSpec((1, d), lambda i: (0, 0)),
                  shp, shs, scp, scs],
        out_specs=pl.BlockSpec((tm, d), lambda i: (i, 0)),
        out_shape=jax.ShapeDtypeStruct((t, d), BF16),
        compiler_params=_params("parallel"),
        name="normmod",
    )(x, g.reshape(1, d), mod_p, mod_s, mod_p, mod_s)


def _rmsnorm_kernel(x_ref, g_ref, o_ref):
    x = x_ref[...]
    o_ref[...] = x * lax.rsqrt(jnp.mean(x * x, axis=-1, keepdims=True) + EPS) * g_ref[...]


def _final_norm(x, g, *, row0, rows, tm):
    d = x.shape[1]
    b0 = row0 // tm
    return pl.pallas_call(
        _rmsnorm_kernel,
        grid=(rows // tm,),
        in_specs=[pl.BlockSpec((tm, d), lambda i: (b0 + i, 0)),
                  pl.BlockSpec((1, d), lambda i: (0, 0))],
        out_specs=pl.BlockSpec((tm, d), lambda i: (i, 0)),
        out_shape=jax.ShapeDtypeStruct((rows, d), F32),
        compiler_params=_params("parallel"),
        name="final_norm",
    )(x, g.reshape(1, d))


def _ffn_kernel(n_ref, x_ref, wg_ref, wu_ref, wd_ref, gp_ref, gs_ref, o_ref, acc_ref, *, npt, gt):
    i, j = pl.program_id(0), pl.program_id(1)

    @pl.when(j == 0)
    def _():
        acc_ref[...] = jnp.zeros_like(acc_ref)

    n = n_ref[...]
    a = _dot(n, wg_ref[...])
    b = _dot(n, wu_ref[...])
    acc_ref[...] += _dot((a * _sigmoid(a) * b).astype(BF16), wd_ref[...])

    @pl.when(j == pl.num_programs(1) - 1)
    def _():
        tm, d = acc_ref.shape
        gate = _pick(i < npt, gp_ref, gs_ref)
        y = (0.5 * gate) * acc_ref[...].reshape(gt, tm // gt, d)
        o_ref[...] = x_ref[...] + y.reshape(tm, d)


def _ffn(n, x, wg, wu, wd, layer, mod_p, mod_s, k_gate, *, tm, npt, gt):
    t, d = x.shape
    f = wg.shape[2]
    tf = 512 if f % 512 == 0 else f
    gp, gs = _mod_specs(k_gate, npt, gt, d)
    return pl.pallas_call(
        functools.partial(_ffn_kernel, npt=npt, gt=gt),
        grid=(t // tm, f // tf),
        in_specs=[pl.BlockSpec((tm, d), lambda i, j: (i, 0)),
                  pl.BlockSpec((tm, d), lambda i, j: (i, 0)),
                  pl.BlockSpec((None, d, tf), lambda i, j: (layer, 0, j)),
                  pl.BlockSpec((None, d, tf), lambda i, j: (layer, 0, j)),
                  pl.BlockSpec((None, tf, d), lambda i, j: (layer, j, 0)),
                  gp, gs],
        out_specs=pl.BlockSpec((tm, d), lambda i, j: (i, 0)),
        out_shape=jax.ShapeDtypeStruct((t, d), F32),
        scratch_shapes=[pltpu.VMEM((tm, d), F32)],
        compiler_params=_params("parallel", "arbitrary"),
        name="ffn",
    )(n, x, wg, wu, wd, mod_p, mod_s)


def _mm_kernel(x_ref, w_ref, *o_refs):
    r = _dot(x_ref[...], w_ref[...])
    for o_ref in o_refs:
        o_ref[...] = r.astype(o_ref.dtype)


def _mm(x, w, out_dtypes, *, tm, tn):
    t, d = x.shape
    n = w.shape[1]
    outs = pl.pallas_call(
        _mm_kernel,
        grid=(t // tm, n // tn),
        in_specs=[pl.BlockSpec((tm, d), lambda i, j: (i, 0)),
                  pl.BlockSpec((d, tn), lambda i, j: (0, j))],
        out_specs=[pl.BlockSpec((tm, tn), lambda i, j: (i, j)) for _ in out_dtypes],
        out_shape=[jax.ShapeDtypeStruct((t, n), dt) for dt in out_dtypes],
        compiler_params=_params("parallel", "arbitrary"),
        name="proj",
    )(x, w)
    return outs


def _kvproj_kernel(x_ref, w_ref, *refs, npt, heads, ct, has_prev):
    kv16_ref, kp_ref, vp_ref, ks_ref, vs_ref = refs[4:] if has_prev else refs
    is_p = pl.program_id(0) < npt
    tm = x_ref.shape[0]
    fw = heads * LANES

    def project(k_dst, v_dst):
        x = x_ref[...]
        for dst_ref, c0 in ((k_dst, 0), (v_dst, fw)):
            for cj in range(fw // ct):
                cols = slice(c0 + cj * ct, c0 + (cj + 1) * ct)
                r = _dot(x, w_ref[:, cols])
                kv16_ref[:, cols] = r.astype(kv16_ref.dtype)
                for hh in range(ct // LANES):
                    dst_ref[pl.ds(cj * (ct // LANES) + hh, tm, stride=heads), :] = (
                        r[:, hh * LANES:(hh + 1) * LANES])

    pl.when(is_p)(functools.partial(project, kp_ref, vp_ref))
    pl.when(jnp.logical_not(is_p))(functools.partial(project, ks_ref, vs_ref))


def _kvproj(x, w, prev, *, layer, depth, heads, np_rows, tm):
    t, d = x.shape
    n = w.shape[1]
    npt = np_rows // tm
    ns_rows = t - np_rows
    fw = heads * LANES
    ct = 512 if fw % 512 == 0 else LANES
    blk = (None, tm * heads, LANES)
    p_spec = pl.BlockSpec(blk, lambda i: (layer, jnp.minimum(i, npt - 1), 0))
    s_spec = pl.BlockSpec(blk, lambda i: (layer, jnp.maximum(i - npt, 0), 0))
    p_shape = jax.ShapeDtypeStruct((depth, np_rows * heads, LANES), F32)
    s_shape = jax.ShapeDtypeStruct((depth, ns_rows * heads, LANES), F32)
    has_prev = prev is not None
    any_spec = pl.BlockSpec(memory_space=pl.ANY)
    return pl.pallas_call(
        functools.partial(_kvproj_kernel, npt=npt, heads=heads, ct=ct, has_prev=has_prev),
        grid=(t // tm,),
        in_specs=[pl.BlockSpec((tm, d), lambda i: (i, 0)),
                  pl.BlockSpec((d, n), lambda i: (0, 0), pipeline_mode=pl.Buffered(1))]
        + ([any_spec] * 4 if has_prev else []),
        out_specs=[pl.BlockSpec((tm, n), lambda i: (i, 0)), p_spec, p_spec, s_spec, s_spec],
        out_shape=[jax.ShapeDtypeStruct((t, n), BF16), p_shape, p_shape, s_shape, s_shape],
        input_output_aliases={2: 1, 3: 2, 4: 3, 5: 4} if has_prev else {},
        compiler_params=_params("arbitrary"),
        name="kvproj",
    )(x, w, *(prev if has_prev else ()))


def _mm_logsig_kernel(x_ref, w_ref, b_ref, o_ref):
    y = _dot(x_ref[...], w_ref[...]) + b_ref[...]
    o_ref[...] = jnp.minimum(y, 0.0) - jnp.log(1.0 + jnp.exp(-jnp.abs(y)))


def _mm_logsig(x, w, b, *, tm):
    t, d = x.shape
    n = w.shape[1]
    return pl.pallas_call(
        _mm_logsig_kernel,
        grid=(t // tm,),
        in_specs=[pl.BlockSpec((tm, d), lambda i: (i, 0)),
                  pl.BlockSpec((d, n), lambda i: (0, 0)),
                  pl.BlockSpec((1, n), lambda i: (0, 0))],
        out_specs=pl.BlockSpec((tm, n), lambda i: (i, 0)),
        out_shape=jax.ShapeDtypeStruct((t, n), F32),
        compiler_params=_params("parallel"),
        name="fox_logf",
    )(x, w, b)


def _merge_kernel(o_ref, y_ref, wa_ref, wb_ref, ga_ref, gb_ref, out_ref):
    a = _dot(o_ref[...], wa_ref[...])
    b = _dot(y_ref[...], wb_ref[...])
    out_ref[...] = (_sigmoid(ga_ref[...]) * a + _sigmoid(gb_ref[...]) * b).astype(out_ref.dtype)


def _merge(o, y, wa, wb, layer, p32, col_ga, col_gb, *, tm, tn):
    t, hw = o.shape
    fw = y.shape[1]
    d = wa.shape[2]
    ja, jb = col_ga // tn, col_gb // tn
    return pl.pallas_call(
        _merge_kernel,
        grid=(t // tm, d // tn),
        in_specs=[pl.BlockSpec((tm, hw), lambda i, j: (i, 0)),
                  pl.BlockSpec((tm, fw), lambda i, j: (i, 0)),
                  pl.BlockSpec((None, hw, tn), lambda i, j: (layer, 0, j)),
                  pl.BlockSpec((None, fw, tn), lambda i, j: (layer, 0, j)),
                  pl.BlockSpec((tm, tn), lambda i, j: (i, ja + j)),
                  pl.BlockSpec((tm, tn), lambda i, j: (i, jb + j))],
        out_specs=pl.BlockSpec((tm, tn), lambda i, j: (i, j)),
        out_shape=jax.ShapeDtypeStruct((t, d), BF16),
        compiler_params=_params("parallel", "arbitrary"),
        name="merge",
    )(o, y, wa, wb, p32, p32)


def _outproj_kernel(m_ref, w_ref, x_ref, gp_ref, gs_ref, o_ref, *, npt, gt):
    r = _dot(m_ref[...], w_ref[...])
    tm, tn = r.shape
    gate = _pick(pl.program_id(0) < npt, gp_ref, gs_ref)
    y = gate * r.reshape(gt, tm // gt, tn)
    o_ref[...] = x_ref[...] + y.reshape(tm, tn)


def _outproj(m, w, layer, x, mod_p, mod_s, k_gate, *, tm, tn, npt, gt):
    t, d = x.shape
    gp, gs = _mod_specs(k_gate, npt, gt, tn, col=True)
    return pl.pallas_call(
        functools.partial(_outproj_kernel, npt=npt, gt=gt),
        grid=(t // tm, d // tn),
        in_specs=[pl.BlockSpec((tm, d), lambda i, j: (i, 0)),
                  pl.BlockSpec((None, d, tn), lambda i, j: (layer, 0, j)),
                  pl.BlockSpec((tm, tn), lambda i, j: (i, j)),
                  gp, gs],
        out_specs=pl.BlockSpec((tm, tn), lambda i, j: (i, j)),
        out_shape=jax.ShapeDtypeStruct((t, d), F32),
        compiler_params=_params("parallel", "arbitrary"),
        name="outproj",
    )(m, w, x, mod_p, mod_s)


def _per_head(fn, hb):
    return jnp.concatenate([fn(slice(h * LANES, (h + 1) * LANES)) for h in range(hb)], axis=1)


def _hgrn_heads(q16, z, v16, ag, sts, lg, hgn, *, layer):
    C, W = z.shape
    hb = W // LANES
    sub = min(HGRN_SUB, C)

    def head_sum(x):
        return _per_head(lambda ln: jnp.broadcast_to(
            jnp.sum(x[:, ln], axis=-1, keepdims=True), (x.shape[0], LANES)), hb)

    e = jnp.exp(lg - jnp.max(lg, axis=0, keepdims=True))
    soft = e / jnp.sum(e, axis=0, keepdims=True)
    lb = jnp.zeros((1, W), F32)
    for i in range(1, layer + 1):
        lb = lb + soft[i:i + 1]

    ez = jnp.exp(-jnp.abs(z))
    r = 1.0 / (1.0 + ez)
    sig_pos = jnp.where(z >= 0, r, ez * r)
    sig_neg = jnp.where(z >= 0, ez * r, r)
    g = jnp.log(lb + (1.0 - lb) * sig_pos) * LOG2E
    kin = (1.0 - lb) * sig_neg

    row = lax.broadcasted_iota(jnp.int32, (C, C), 0)
    col = lax.broadcasted_iota(jnp.int32, (C, C), 1)
    tri = (row >= col).astype(BF16)
    g1, g2, g3 = _split3(g)
    cum = _dot(tri, g1) + _dot(tri, g2) + _dot(tri, g3)
    cum_k = cum - jnp.log(kin) * LOG2E

    q = q16.astype(F32)
    v = v16.astype(F32)
    qe_all = (q * jnp.exp2(cum)).astype(BF16)
    sts16 = [st.astype(BF16) for st in sts]
    nsub = C // sub
    o_inter = _per_head(lambda ln: _dot_nt(qe_all[:, ln], sts16[ln.start // LANES]), hb)
    att = {}
    for i in range(1, nsub):
        r0 = i * sub
        b_i = cum[r0 - 1:r0]
        qe = (q[r0:r0 + sub] * jnp.exp2(cum[r0:r0 + sub] - b_i)).astype(BF16)
        ke = (kin[:r0] * jnp.exp2(b_i - cum[:r0])).astype(BF16)
        for h in range(hb):
            ln = slice(h * LANES, (h + 1) * LANES)
            att[i, h] = _dot_nt(qe[:, ln], ke[:, ln]).astype(BF16)
    o_off = {i: _per_head(lambda ln: _dot(att[i, ln.start // LANES], v16[:i * sub, ln]), hb)
             for i in range(1, nsub)}

    row8 = lax.broadcasted_iota(jnp.int32, (SUBLANES, 1), 0)
    outs = []
    for i in range(nsub):
        r0 = i * sub
        acc = o_inter[r0:r0 + sub]
        if i > 0:
            acc = acc + o_off[i]
        for u in range(sub // SUBLANES):
            t0 = r0 + u * SUBLANES
            cum_t, q_t = cum[t0:t0 + SUBLANES], q[t0:t0 + SUBLANES]
            acc_t = acc[u * SUBLANES:(u + 1) * SUBLANES]
            for s in range(r0, t0 + SUBLANES):
                d = cum_t - cum_k[s:s + 1]
                if s >= t0:
                    d = jnp.where(row8 >= s - t0, d, MASK_VALUE)
                acc_t = acc_t + head_sum(q_t * jnp.exp2(d)) * v[s:s + 1]
            outs.append(acc_t)
    o = jnp.concatenate(outs, axis=0)

    o = o * lax.rsqrt(head_sum(o * o) * (1.0 / LANES) + EPS) * hgn
    agf = ag.astype(F32)
    o = o * (agf * _sigmoid(agf))

    last = cum[C - 1:C]
    kdec = (kin * jnp.exp2(last - cum)).astype(BF16)
    decay = jnp.exp2(last)
    new_sts = [sts[h] * decay[:, h * LANES:(h + 1) * LANES]
               + _dot_tn(v16[:, h * LANES:(h + 1) * LANES], kdec[:, h * LANES:(h + 1) * LANES])
               for h in range(hb)]
    return o, new_sts


def _hgrn_kernel(q_ref, z_ref, v_ref, ag_ref, s0_ref, lbl_ref, hgn_ref, *refs, layer, nc, hb, has_prev):
    o_ref, sout_ref, st_ref = refs[1:] if has_prev else refs
    c = pl.program_id(2)

    @pl.when(c == 0)
    def _():
        for hh in range(hb):
            st_ref[hh] = s0_ref[hh].T

    o, new_sts = _hgrn_heads(q_ref[...], z_ref[...], v_ref[...], ag_ref[...],
                             [st_ref[hh] for hh in range(hb)], lbl_ref[...], hgn_ref[...], layer=layer)
    o_ref[...] = o.astype(o_ref.dtype)
    for hh in range(hb):
        st_ref[hh] = new_sts[hh]

    @pl.when(c == nc - 1)
    def _():
        for hh in range(hb):
            sout_ref[hh] = st_ref[hh].T


def _hgrn(p16, p32, s0, lb_logits, hg_norm, prev, *, layer, row0, nb, length, chunk, hb,
          col_q, col_v, col_z, col_ag):
    heads = s0.shape[1]
    nc = length // chunk
    rb0 = row0 // chunk
    depth = lb_logits.shape[0]
    w = hb * LANES
    has_prev = prev is not None

    def rows(b, h, c):
        return rb0 + b * nc + c

    return pl.pallas_call(
        functools.partial(_hgrn_kernel, layer=layer, nc=nc, hb=hb, has_prev=has_prev),
        grid=(nb, heads // hb, nc),
        in_specs=[pl.BlockSpec((chunk, w), lambda b, h, c: (rows(b, h, c), col_q // hb + h)),
                  pl.BlockSpec((chunk, w), lambda b, h, c: (rows(b, h, c), col_z // hb + h)),
                  pl.BlockSpec((chunk, w), lambda b, h, c: (rows(b, h, c), col_v // hb + h)),
                  pl.BlockSpec((chunk, w), lambda b, h, c: (rows(b, h, c), col_ag // hb + h)),
                  pl.BlockSpec((None, hb, LANES, LANES), lambda b, h, c: (b, h, 0, 0)),
                  pl.BlockSpec((depth, w), lambda b, h, c: (0, h)),
                  pl.BlockSpec((1, w), lambda b, h, c: (0, h))]
        + ([pl.BlockSpec(memory_space=pl.ANY)] if has_prev else []),
        out_specs=[pl.BlockSpec((chunk, w), lambda b, h, c: (rows(b, h, c), h)),
                   pl.BlockSpec((None, hb, LANES, LANES), lambda b, h, c: (b, h, 0, 0))],
        out_shape=[jax.ShapeDtypeStruct((p16.shape[0], heads * LANES), BF16),
                   jax.ShapeDtypeStruct((nb, heads, LANES, LANES), F32)],
        input_output_aliases={7: 0} if has_prev else {},
        scratch_shapes=[pltpu.VMEM((hb, LANES, LANES), F32)],
        compiler_params=_params("parallel", "parallel", "arbitrary"),
        name="hgrn",
    )(p16, p32, p16, p32, s0, lb_logits, hg_norm, *([prev] if has_prev else []))


def _cumsum_kernel(x_ref, o_ref, carry_ref):
    @pl.when(pl.program_id(0) == 0)
    def _():
        carry_ref[...] = jnp.zeros_like(carry_ref)

    x = x_ref[...]
    bw = x.shape[1]
    row = lax.broadcasted_iota(jnp.int32, (bw, bw), 0)
    col = lax.broadcasted_iota(jnp.int32, (bw, bw), 1)
    upper = (row <= col).astype(BF16)
    x1, x2, x3 = _split3(x)
    c = _dot(x1, upper) + _dot(x2, upper) + _dot(x3, upper) + carry_ref[...]
    o_ref[...] = c
    carry_ref[...] = c[:, bw - 1:bw]


def _cumsum_lanes(x):
    r, n = x.shape
    bw = 256 if n % 256 == 0 else LANES
    return pl.pallas_call(
        _cumsum_kernel,
        grid=(n // bw,),
        in_specs=[pl.BlockSpec((r, bw), lambda j: (0, j))],
        out_specs=pl.BlockSpec((r, bw), lambda j: (0, j)),
        out_shape=jax.ShapeDtypeStruct((r, n), F32),
        scratch_shapes=[pltpu.VMEM((r, 1), F32)],
        compiler_params=_params("arbitrary"),
        name="cumsum",
    )(x)


def _fox_update(s, v, m_ref, l_ref, acc_ref, slot, masked):
    tq, tk = s.shape
    if masked:
        row = lax.broadcasted_iota(jnp.int32, (tq, tk), 0)
        col = lax.broadcasted_iota(jnp.int32, (tq, tk), 1)
        s = jnp.where(row >= col, s, MASK_VALUE)
    m_old = m_ref[slot]
    m_new = jnp.maximum(m_old, jnp.max(s, axis=-1, keepdims=True))
    alpha = jnp.exp2(m_old - m_new)
    if tk % LANES == 0:
        ps = [jnp.exp2(s[:, c * LANES:(c + 1) * LANES] - m_new) for c in range(tk // LANES)]
        p_lanes = functools.reduce(jnp.add, ps)
        p = jnp.concatenate(ps, axis=1)
    else:
        p = jnp.exp2(s - m_new[:, :tk])
        p_lanes = jnp.concatenate([p, jnp.zeros((tq, LANES - tk), F32)], axis=1)
    l_ref[slot] = alpha * l_ref[slot] + p_lanes
    acc_ref[slot] = alpha * acc_ref[slot] + _dot(p.astype(BF16), v)
    m_ref[slot] = m_new


def _fox_init(m_ref, l_ref, acc_ref):
    m_ref[...] = jnp.full_like(m_ref, MASK_VALUE)
    l_ref[...] = jnp.zeros_like(l_ref)
    acc_ref[...] = jnp.zeros_like(acc_ref)


def _fox_out(l_ref, acc_ref, slot):
    return acc_ref[slot] / jnp.sum(l_ref[slot], axis=-1, keepdims=True)


def _fox_tile(q, k, v, bias, m_ref, l_ref, acc_ref, slot, masked):
    _fox_update(_dot_nt(q, k) + bias, v, m_ref, l_ref, acc_ref, slot, masked)


def _fox_prompt_kernel(q_ref, k_ref, v_ref, c_ref, o_ref, s_ref, m_ref, l_ref, acc_ref, kn_ref, *, tb, hp):
    qi = pl.program_id(1)
    nk = k_ref.shape[0] // tb
    chains = [(hh, half) for hh in range(hp) for half in range(2)]

    def lanes(hh):
        return slice(hh * LANES, (hh + 1) * LANES)

    @pl.when(qi == 0)
    def _():
        def kn_body(i, mx):
            kb = k_ref[pl.ds(pl.multiple_of(i * tb, tb), tb), :].astype(F32)
            for hh in range(hp):
                kh = kb[:, lanes(hh)]
                mx = jnp.maximum(mx, jnp.max(jnp.sum(kh * kh, axis=-1, keepdims=True), axis=0, keepdims=True))
            return mx
        kn_ref[...] = jnp.broadcast_to(lax.fori_loop(0, nk, kn_body, jnp.zeros((1, 1), F32)), kn_ref.shape)

    _fox_init(m_ref, l_ref, acc_ref)
    base = pl.multiple_of(qi * (2 * tb), 2 * tb)
    c_first = [c_ref[hh, :, pl.ds(base, LANES)][:, 0:1] for hh in range(hp)]
    qf = q_ref[...].astype(F32)
    qn2 = jnp.zeros((1, 1), F32)
    for hh in range(hp):
        qh = qf[:, lanes(hh)]
        qn2 = jnp.maximum(qn2, jnp.max(jnp.sum(qh * qh, axis=-1, keepdims=True), axis=0, keepdims=True))
    qk_max = jnp.sqrt(qn2 * kn_ref[:, 0:1])

    def scores(ch, slot, kstart):
        hh, half = chains[ch]
        bias = (c_first[hh] - c_ref[hh, :, pl.ds(kstart, tb)]) * LOG2E
        s_ref[ch, slot] = _dot_nt(q_ref[half * tb:(half + 1) * tb, lanes(hh)],
                                  k_ref[pl.ds(kstart, tb), lanes(hh)]) + bias

    def update(ch, slot, kstart, masked):
        hh, _ = chains[ch]
        _fox_update(s_ref[ch, slot], v_ref[pl.ds(kstart, tb), lanes(hh)], m_ref, l_ref, acc_ref, ch, masked)

    def tile_start(i):
        return pl.multiple_of(jnp.maximum(i, 0) * tb, tb)

    def live(kstart):
        bias_max = None
        for hh in range(hp):
            c_last = c_ref[hh, :, pl.ds(kstart + (tb - LANES), LANES)][:, LANES - 1:LANES]
            b = (c_first[hh] - c_last) * LOG2E
            bias_max = b if bias_max is None else jnp.maximum(bias_max, b)
        s_max = qk_max * (1.0 + 2.0 ** -6) + bias_max
        m_min = jnp.min(jnp.min(m_ref[...], axis=0), axis=0, keepdims=True)[:, 0:1]
        return jnp.where(s_max > m_min - EXP2_ZERO, 1, 0)[0, 0]

    top = pl.multiple_of(base + tb, tb)
    for ch, (hh, half) in enumerate(chains):
        if half == 1:
            scores(ch, 1, top)
        scores(ch, 0, base)
    for ch, (hh, half) in enumerate(chains):
        if half == 1:
            update(ch, 1, top, True)
        update(ch, 0, base, half == 0)

    n = 2 * qi
    for ch in range(len(chains)):
        scores(ch, 0, tile_start(n - 1))

    def body(carry):
        j, _ = carry
        k0, k1, k2 = tile_start(n - 1 - 2 * j), tile_start(n - 2 - 2 * j), tile_start(n - 3 - 2 * j)
        for ch in range(len(chains)):
            scores(ch, 1, k1)
            update(ch, 0, k0, False)
        for ch in range(len(chains)):
            scores(ch, 0, k2)
            update(ch, 1, k1, False)
        return j + 1, live(k2)

    lax.while_loop(lambda carry: jnp.logical_and(carry[0] < qi, carry[1] > 0), body,
                   (jnp.int32(0), live(tile_start(n - 1))))
    for ch, (hh, half) in enumerate(chains):
        o_ref[half * tb:(half + 1) * tb, lanes(hh)] = _fox_out(l_ref, acc_ref, ch).astype(o_ref.dtype)


def _fox_prompt(p16, kv16, c, *, heads, length, col_q, col_k, col_v, tb, hp):
    w = hp * LANES
    stat = pltpu.VMEM((2 * hp, tb, LANES), F32)
    return pl.pallas_call(
        functools.partial(_fox_prompt_kernel, tb=tb, hp=hp),
        grid=(heads // hp, length // (2 * tb)),
        in_specs=[pl.BlockSpec((2 * tb, w), lambda h, qi: (qi, col_q // hp + h)),
                  pl.BlockSpec((length, w), lambda h, qi: (0, col_k // hp + h)),
                  pl.BlockSpec((length, w), lambda h, qi: (0, col_v // hp + h)),
                  pl.BlockSpec((hp, 1, length), lambda h, qi: (h, 0, 0))],
        out_specs=pl.BlockSpec((2 * tb, w), lambda h, qi: (qi, h)),
        out_shape=jax.ShapeDtypeStruct((p16.shape[0], heads * LANES), BF16),
        scratch_shapes=[pltpu.VMEM((2 * hp, 2, tb, tb), F32), stat, stat, stat,
                        pltpu.VMEM((1, LANES), F32)],
        compiler_params=_params("parallel", "arbitrary"),
        name="fox_prompt",
    )(p16, kv16, kv16, c)


def _fox_sample_kernel(q_ref, kn_ref, vn_ref, kc_ref, vc_ref, cq_ref, ckc_ref, ckn_ref, ckl_ref, _, o_ref,
                       m_ref, l_ref, acc_ref, q1_ref, *, heads, nck):
    c = pl.program_id(1)
    tkc = kc_ref.shape[0] // heads
    cq_all = cq_ref[...] * LOG2E

    def head_lanes(h):
        return slice(h * LANES, (h + 1) * LANES)

    @pl.when(c == 0)
    def _():
        _fox_init(m_ref, l_ref, acc_ref)
        s_new = [_dot_nt(q_ref[:, head_lanes(h)], kn_ref[:, head_lanes(h)])
                 + (cq_all[:, h:h + 1] - ckn_ref[h] * LOG2E) for h in range(heads)]
        for h in range(heads):
            _fox_update(s_new[h], vn_ref[:, head_lanes(h)], m_ref, l_ref, acc_ref, h, True)
            q = q_ref[:, head_lanes(h)].astype(F32)
            q1 = jnp.max(jnp.sum(jnp.abs(q), axis=-1, keepdims=True), axis=0, keepdims=True)
            q1_ref[h] = jnp.broadcast_to(q1, (1, LANES))

    k_max = jnp.max(jnp.max(jnp.abs(kc_ref[...]), axis=0, keepdims=True), axis=-1, keepdims=True)
    cq_max = jnp.max(cq_all, axis=0, keepdims=True)
    c_last = ckl_ref[...] * LOG2E
    live = jnp.zeros((1, 1), jnp.int32)
    for h in range(heads):
        s_max = q1_ref[h][:, 0:1] * k_max * (1.0 + 2.0 ** -6) + (cq_max[:, h:h + 1] - c_last[:, h:h + 1])
        m_min = jnp.min(m_ref[h], axis=0, keepdims=True)[:, 0:1]
        live = jnp.maximum(live, jnp.where(s_max > m_min - EXP2_ZERO, 1, 0))

    @pl.when(live[0, 0] > 0)
    def _():
        s_all = []
        for h in range(heads):
            k = kc_ref[pl.ds(h, tkc, stride=heads), :].astype(BF16)
            s_all.append(_dot_nt(q_ref[:, head_lanes(h)], k) + (cq_all[:, h:h + 1] - ckc_ref[h] * LOG2E))
        for h in range(heads):
            v = vc_ref[pl.ds(h, tkc, stride=heads), :].astype(BF16)
            _fox_update(s_all[h], v, m_ref, l_ref, acc_ref, h, False)

    @pl.when(c == nck - 1)
    def _():
        for h in range(heads):
            o_ref[:, head_lanes(h)] = _fox_out(l_ref, acc_ref, h).astype(o_ref.dtype)


def _fox_sample(p16, kv16, cache_k, cache_v, cq, ckc, ckn, prev, *, layer, row0, nb, ls, heads,
                col_q, col_k, col_v, tkc):
    past = cache_k.shape[2] // heads
    nck = past // tkc
    rb0 = row0 // ls
    w = heads * LANES
    stat = pltpu.VMEM((heads, ls, LANES), F32)
    ckl = ckc[:, :, 0, tkc - 1::tkc].transpose(0, 2, 1)[:, :, None, :]

    def chunk(c):
        return nck - 1 - c

    return pl.pallas_call(
        functools.partial(_fox_sample_kernel, heads=heads, nck=nck),
        grid=(nb, nck),
        in_specs=[pl.BlockSpec((ls, w), lambda b, c: (rb0 + b, col_q // heads)),
                  pl.BlockSpec((ls, w), lambda b, c: (rb0 + b, col_k // heads)),
                  pl.BlockSpec((ls, w), lambda b, c: (rb0 + b, col_v // heads)),
                  pl.BlockSpec((None, None, tkc * heads, LANES), lambda b, c: (layer, b, chunk(c), 0)),
                  pl.BlockSpec((None, None, tkc * heads, LANES), lambda b, c: (layer, b, chunk(c), 0)),
                  pl.BlockSpec((None, ls, heads), lambda b, c: (b, 0, 0)),
                  pl.BlockSpec((None, heads, 1, tkc), lambda b, c: (b, 0, 0, chunk(c))),
                  pl.BlockSpec((None, heads, 1, ls), lambda b, c: (b, 0, 0, 0)),
                  pl.BlockSpec((None, None, 1, heads), lambda b, c: (b, chunk(c), 0, 0)),
                  pl.BlockSpec(memory_space=pl.ANY)],
        out_specs=pl.BlockSpec((ls, w), lambda b, c: (rb0 + b, 0)),
        out_shape=jax.ShapeDtypeStruct(prev.shape, BF16),
        input_output_aliases={9: 0},
        scratch_shapes=[stat, stat, stat, pltpu.VMEM((heads, 1, LANES), F32)],
        compiler_params=_params("parallel", "arbitrary"),
        name="fox_sample",
    )(p16, kv16, kv16, cache_k, cache_v, cq, ckc, ckn, ckl, prev)


def _row_tile(np_rows, ns_rows, ls):
    for tm in (512, 256, 128, 64, 32):
        if np_rows % tm == 0 and ns_rows % tm == 0 and tm % ls == 0:
            return tm
    raise ValueError("no row tile fits the prompt / sample row counts")


def kernel(x_prompt, x_sample, c_prompt, c_sample, cache_fox_k, cache_fox_v, cache_fox_logf,
           state_hgrn, ada_w, ada_b, ln_ffn1, ln_mix, ln_ffn2, ffn1_w_gate, ffn1_w_up,
           ffn1_w_down, ffn2_w_gate, ffn2_w_up, ffn2_w_down, w_in, hgrn_lb_logits,
           hgrn_norm_g, fox_f_bias, w_branch_a, w_branch_b, w_out, ln_final):
    bp, lp, d = x_prompt.shape
    bs, ls, _ = x_sample.shape
    depth = ada_w.shape[0]
    _, _, past, fh, fdh = cache_fox_k.shape
    _, _, hh, dk, dv = state_hgrn.shape
    assert bp == 1 and dk == LANES and dv == LANES and fdh == LANES
    assert ls % SUBLANES == 0 and d % LANES == 0
    np_rows, ns_rows = bp * lp, bs * ls
    t = np_rows + ns_rows
    hw, fw = hh * LANES, fh * LANES
    tm = _row_tile(np_rows, ns_rows, ls)
    npt, gt = np_rows // tm, tm // ls
    tn = 512 if (d % 512 == 0 and hw % 512 == 0 and fw % 512 == 0) else LANES
    chunk_p = 64 if lp % 64 == 0 else lp
    tb = 256 if lp % 512 == 0 else lp // 2
    tkc = 512 if past % 512 == 0 else past
    hb = next(n for n in (8, 4, 2, 1) if hh % n == 0)
    tm_kv = min(tm, 256)
    tm_big = 2 * tm if (np_rows % (2 * tm) == 0 and ns_rows % (2 * tm) == 0) else tm
    tiles = dict(tm=tm, npt=npt, gt=gt)
    tiles_big = dict(tm=tm_big, npt=np_rows // tm_big, gt=tm_big // ls)
    q_scale = LOG2E * float(fdh) ** -0.5

    x = jnp.concatenate([x_prompt.reshape(np_rows, d), x_sample.reshape(ns_rows, d)], axis=0)
    c_all = jnp.concatenate([c_prompt, c_sample], axis=0)
    m_pad = -(-c_all.shape[0] // SUBLANES) * SUBLANES
    c_pad = jnp.pad(c_all, ((0, m_pad - c_all.shape[0]), (0, 0)))
    s0_prompt = jnp.zeros((bp, hh, dk, dv), F32)

    sizes = (hw, hw, hw, hw, fw, fw, fw, fh, d, d)
    offs = [0]
    for s in sizes:
        offs.append(offs[-1] + s)
    o_aq, o_af, o_ai, o_ag, o_fq, o_fk, o_fv, o_ffl, o_ga, o_gb = offs[:10]

    outs = {k: [] for k in ("lfp", "sp", "lfs", "ss")}
    cache_k = cache_fox_k.reshape(depth, bs, past * fh, fdh)
    cache_v = cache_fox_v.reshape(depth, bs, past * fh, fdh)
    kv_out = None
    f1g, f1u, f1d = ffn1_w_gate.astype(BF16), ffn1_w_up.astype(BF16), ffn1_w_down.astype(BF16)
    f2g, f2u, f2d = ffn2_w_gate.astype(BF16), ffn2_w_up.astype(BF16), ffn2_w_down.astype(BF16)
    wba, wbb, wo = w_branch_a.astype(BF16), w_branch_b.astype(BF16), w_out.astype(BF16)
    h = x
    for l in range(depth):
        wi = w_in[l]
        w16 = jnp.concatenate([wi[:, o_aq:o_aq + hw], wi[:, o_ai:o_ai + hw],
                               wi[:, o_fq:o_fq + fw] * q_scale], axis=1).astype(BF16)
        w32 = jnp.concatenate([wi[:, o_af:o_af + hw], wi[:, o_ag:o_ag + hw],
                               wi[:, o_ga:o_ga + d], wi[:, o_gb:o_gb + d]], axis=1).astype(BF16)
        wkv = wi[:, o_fk:o_fk + 2 * fw].astype(BF16)
        wffl = jnp.pad(wi[:, o_ffl:o_ffl + fh], ((0, 0), (0, LANES - fh))).astype(BF16)
        bffl = jnp.pad(fox_f_bias[l], (0, LANES - fh)).reshape(1, LANES)

        mod = _ada(c_pad, ada_w, ada_b, l)
        mod_p = mod[0:bp].reshape(bp, N_MOD, 1, d)
        mod_s = mod[bp:bp + bs].reshape(bs, N_MOD, 1, d)

        n1 = _normmod(h, ln_ffn1[l], mod_p, mod_s, 0, 1, **tiles)
        h = _ffn(n1, h, f1g, f1u, f1d, l, mod_p, mod_s, 2, **tiles)

        n2 = _normmod(h, ln_mix[l], mod_p, mod_s, 3, 4, **tiles)
        (p16,) = _mm(n2, w16, (BF16,), tm=tm_big, tn=tn)
        (p32,) = _mm(n2, w32, (F32,), tm=tm_big, tn=tn)
        kv16, *kv_out = _kvproj(n2, wkv, kv_out, layer=l, depth=depth, heads=fh,
                                np_rows=np_rows, tm=tm_kv)
        lf_pad = _mm_logsig(n2, wffl, bffl, tm=tm)

        cb = lambda off: off // LANES
        hg_cols = dict(col_q=0, col_v=cb(hw), col_z=0, col_ag=cb(hw), hb=hb)
        o_all, s_p = _hgrn(p16, p32, s0_prompt, hgrn_lb_logits, hgrn_norm_g[l].reshape(1, hw), None,
                           layer=l, row0=0, nb=bp, length=lp, chunk=chunk_p, **hg_cols)
        o_all, s_s = _hgrn(p16, p32, state_hgrn[l], hgrn_lb_logits, hgrn_norm_g[l].reshape(1, hw), o_all,
                           layer=l, row0=np_rows, nb=bs, length=ls, chunk=ls, **hg_cols)

        lf = lf_pad[:, :fh]
        lf_p, lf_s = lf[:np_rows], lf[np_rows:]
        c_p = _cumsum_lanes(lf_p.T)
        cache_t = cache_fox_logf[l].transpose(0, 2, 1).reshape(bs * fh, past)
        new_t = lf_s.reshape(bs, ls, fh).transpose(0, 2, 1).reshape(bs * fh, ls)
        n_all = -(-(past + ls) // LANES) * LANES
        c_s = _cumsum_lanes(jnp.concatenate(
            [cache_t, new_t, jnp.zeros((bs * fh, n_all - past - ls), F32)], axis=1))
        cq_s = c_s[:, past:past + ls].reshape(bs, fh, ls).transpose(0, 2, 1)
        ckc_s = c_s[:, :past].reshape(bs, fh, 1, past)
        ckn_s = c_s[:, past:past + ls].reshape(bs, fh, 1, ls)

        fox_cols = dict(col_q=cb(2 * hw), col_k=0, col_v=cb(fw))
        y_all = _fox_prompt(p16, kv16, c_p.reshape(fh, 1, lp), heads=fh, length=lp, tb=tb,
                            hp=2 if fh % 2 == 0 else 1, **fox_cols)
        y_all = _fox_sample(p16, kv16, cache_k, cache_v, cq_s, ckc_s, ckn_s, y_all, layer=l,
                            row0=np_rows, nb=bs, ls=ls, heads=fh, tkc=tkc, **fox_cols)

        merged = _merge(o_all, y_all, wba, wbb, l, p32, 2 * hw, 2 * hw + d, tm=tm_big, tn=tn)
        h = _outproj(merged, wo, l, h, mod_p, mod_s, 5, tn=tn, **tiles_big)

        n3 = _normmod(h, ln_ffn2[l], mod_p, mod_s, 6, 7, **tiles)
        h = _ffn(n3, h, f2g, f2u, f2d, l, mod_p, mod_s, 8, **tiles)

        outs["lfp"].append(lf_p.reshape(bp, lp, fh))
        outs["sp"].append(s_p)
        outs["lfs"].append(lf_s.reshape(bs, ls, fh))
        outs["ss"].append(s_s)

    y_prompt = _final_norm(h, ln_final, row0=0, rows=np_rows, tm=tm).reshape(bp, lp, d)
    y_sample = _final_norm(h, ln_final, row0=np_rows, rows=ns_rows, tm=tm).reshape(bs, ls, d)
    st = {k: jnp.stack(v) for k, v in outs.items()}
    kp, vp, ks, vs = kv_out
    return (y_prompt, y_sample, kp.reshape(depth, bp, lp, fh, fdh), vp.reshape(depth, bp, lp, fh, fdh),
            st["lfp"], st["sp"], ks.reshape(depth, bs, ls, fh, fdh), vs.reshape(depth, bs, ls, fh, fdh),
            st["lfs"], st["ss"])
```

```python
import functools

import jax
import jax.numpy as jnp
from jax import lax
from jax.experimental import pallas as pl
from jax.experimental.pallas import tpu as pltpu

EPS = 1e-6
MASK_VALUE = -1e30
N_MOD = 9
LANES = 128
SUBLANES = 8
V7X_VMEM_BYTES = 64 * 1024 * 1024
VMEM_LIMIT = V7X_VMEM_BYTES * 3 // 4
HGRN_SUB = 16
LOG2E = 1.4426950408889634
EXP2_ZERO = 152.0

F32 = jnp.float32
BF16 = jnp.bfloat16


def _params(*sem):
    return pltpu.CompilerParams(dimension_semantics=sem, vmem_limit_bytes=VMEM_LIMIT)


def _sigmoid(x):
    return 1.0 / (1.0 + jnp.exp(-x))


def _dot(a, b):
    return jnp.dot(a, b, preferred_element_type=F32)


def _dot_nt(a, b):
    return lax.dot_general(a, b, (((1,), (1,)), ((), ())), preferred_element_type=F32)


def _dot_tn(a, b):
    return lax.dot_general(a, b, (((0,), (0,)), ((), ())), preferred_element_type=F32)


def _split3(x):
    x1 = x.astype(BF16)
    r1 = x - x1.astype(F32)
    x2 = r1.astype(BF16)
    x3 = (r1 - x2.astype(F32)).astype(BF16)
    return x1, x2, x3


def _pick(is_prompt, p_ref, s_ref):
    return jnp.where(is_prompt, p_ref[...][None], s_ref[...])


def _mod_specs(k, npt, gt, n, col=None):
    if col is None:
        p = pl.BlockSpec((None, None, 1, n), lambda i, *_: (0, k, 0, 0))
        s = pl.BlockSpec((gt, None, 1, n), lambda i, *_: (jnp.maximum(i - npt, 0), k, 0, 0))
    else:
        p = pl.BlockSpec((None, None, 1, n), lambda i, j: (0, k, 0, j))
        s = pl.BlockSpec((gt, None, 1, n), lambda i, j: (jnp.maximum(i - npt, 0), k, 0, j))
    return p, s


def _ada_kernel(c_ref, w_ref, b_ref, o_ref):
    c = c_ref[...]
    a = (c * _sigmoid(c)).astype(BF16)
    o_ref[...] = _dot(a, w_ref[...].astype(BF16)) + b_ref[...]


def _ada(c_pad, w, b, layer):
    m, d = c_pad.shape
    depth, _, n = w.shape
    tn = 1024 if n % 1024 == 0 else n
    return pl.pallas_call(
        _ada_kernel,
        grid=(n // tn,),
        in_specs=[pl.BlockSpec((m, d), lambda j: (0, 0)),
                  pl.BlockSpec((d, tn), lambda j: (layer, j)),
                  pl.BlockSpec((None, 1, tn), lambda j: (layer, 0, j))],
        out_specs=pl.BlockSpec((m, tn), lambda j: (0, j)),
        out_shape=jax.ShapeDtypeStruct((m, n), F32),
        compiler_params=_params("arbitrary"),
        name="ada",
    )(c_pad, w.reshape(depth * d, n), b.reshape(depth, 1, n))


def _normmod_kernel(x_ref, g_ref, shp_ref, shs_ref, scp_ref, scs_ref, o_ref, *, npt, gt):
    is_p = pl.program_id(0) < npt
    x = x_ref[...]
    tm, d = x.shape
    y = x * lax.rsqrt(jnp.mean(x * x, axis=-1, keepdims=True) + EPS) * g_ref[...]
    sh = _pick(is_p, shp_ref, shs_ref)
    sc = _pick(is_p, scp_ref, scs_ref)
    y = y.reshape(gt, tm // gt, d) * (1.0 + sc) + sh
    o_ref[...] = y.reshape(tm, d).astype(o_ref.dtype)


def _normmod(x, g, mod_p, mod_s, k_shift, k_scale, *, tm, npt, gt):
    t, d = x.shape
    shp, shs = _mod_specs(k_shift, npt, gt, d)
    scp, scs = _mod_specs(k_scale, npt, gt, d)
    return pl.pallas_call(
        functools.partial(_normmod_kernel, npt=npt, gt=gt),
        grid=(t // tm,),
        in_specs=[pl.BlockSpec((tm, d), lambda i: (i, 0)),
                  pl.BlockSpec((1, d), lambda i: (0, 0)),
                  shp, shs, scp, scs],
        out_specs=pl.BlockSpec((tm, d), lambda i: (i, 0)),
        out_shape=jax.ShapeDtypeStruct((t, d), BF16),
        compiler_params=_params("parallel"),
        name="normmod",
    )(x, g.reshape(1, d), mod_p, mod_s, mod_p, mod_s)


def _rmsnorm_kernel(x_ref, g_ref, o_ref):
    x = x_ref[...]
    o_ref[...] = x * lax.rsqrt(jnp.mean(x * x, axis=-1, keepdims=True) + EPS) * g_ref[...]


def _final_norm(x, g, *, row0, rows, tm):
    d = x.shape[1]
    b0 = row0 // tm
    return pl.pallas_call(
        _rmsnorm_kernel,
        grid=(rows // tm,),
        in_specs=[pl.BlockSpec((tm, d), lambda i: (b0 + i, 0)),
                  pl.BlockSpec((1, d), lambda i: (0, 0))],
        out_specs=pl.BlockSpec((tm, d), lambda i: (i, 0)),
        out_shape=jax.ShapeDtypeStruct((rows, d), F32),
        compiler_params=_params("parallel"),
        name="final_norm",
    )(x, g.reshape(1, d))


def _ffn_kernel(n_ref, x_ref, wg_ref, wu_ref, wd_ref, gp_ref, gs_ref, o_ref, acc_ref, *, npt, gt):
    i, j = pl.program_id(0), pl.program_id(1)

    @pl.when(j == 0)
    def _():
        acc_ref[...] = jnp.zeros_like(acc_ref)

    n = n_ref[...]
    a = _dot(n, wg_ref[...])
    b = _dot(n, wu_ref[...])
    acc_ref[...] += _dot((a * _sigmoid(a) * b).astype(BF16), wd_ref[...])

    @pl.when(j == pl.num_programs(1) - 1)
    def _():
        tm, d = acc_ref.shape
        gate = _pick(i < npt, gp_ref, gs_ref)
        y = (0.5 * gate) * acc_ref[...].reshape(gt, tm // gt, d)
        o_ref[...] = x_ref[...] + y.reshape(tm, d)


def _ffn(n, x, wg, wu, wd, layer, mod_p, mod_s, k_gate, *, tm, npt, gt):
    t, d = x.shape
    f = wg.shape[2]
    tf = 512 if f % 512 == 0 else f
    gp, gs = _mod_specs(k_gate, npt, gt, d)
    return pl.pallas_call(
        functools.partial(_ffn_kernel, npt=npt, gt=gt),
        grid=(t // tm, f // tf),
        in_specs=[pl.BlockSpec((tm, d), lambda i, j: (i, 0)),
                  pl.BlockSpec((tm, d), lambda i, j: (i, 0)),
                  pl.BlockSpec((None, d, tf), lambda i, j: (layer, 0, j)),
                  pl.BlockSpec((None, d, tf), lambda i, j: (layer, 0, j)),
                  pl.BlockSpec((None, tf, d), lambda i, j: (layer, j, 0)),
                  gp, gs],
        out_specs=pl.BlockSpec((tm, d), lambda i, j: (i, 0)),
        out_shape=jax.ShapeDtypeStruct((t, d), F32),
        scratch_shapes=[pltpu.VMEM((tm, d), F32)],
        compiler_params=_params("parallel", "arbitrary"),
        name="ffn",
    )(n, x, wg, wu, wd, mod_p, mod_s)


def _mm_kernel(x_ref, w_ref, o_ref, *, scale_from, scale):
    r = _dot(x_ref[...], w_ref[...])
    if scale_from is not None:
        r = r * jnp.where(pl.program_id(1) >= scale_from, scale, 1.0)
    o_ref[...] = r.astype(o_ref.dtype)


def _mm(x, w, out_dtype, groups, *, layer, tm, tn, scale_group=None, scale=1.0):
    t, d = x.shape
    starts = [s // tn for s, _ in groups]
    widths = [w_ // tn for _, w_ in groups]
    assert all(s % tn == 0 and w_ % tn == 0 for s, w_ in groups)
    out_starts = [sum(widths[:g]) for g in range(len(groups))]
    n = sum(widths) * tn

    def wcol(j):
        col = j + (starts[0] - out_starts[0])
        for g in range(1, len(groups)):
            step = (starts[g] - out_starts[g]) - (starts[g - 1] - out_starts[g - 1])
            col = col + jnp.where(j >= out_starts[g], step, 0)
        return col

    assert scale_group is None or scale_group == len(groups) - 1
    scale_from = None if scale_group is None else out_starts[scale_group]
    return pl.pallas_call(
        functools.partial(_mm_kernel, scale_from=scale_from, scale=scale),
        grid=(t // tm, n // tn),
        in_specs=[pl.BlockSpec((tm, d), lambda i, j: (i, 0)),
                  pl.BlockSpec((None, d, tn), lambda i, j: (layer, 0, wcol(j)))],
        out_specs=pl.BlockSpec((tm, tn), lambda i, j: (i, j)),
        out_shape=jax.ShapeDtypeStruct((t, n), out_dtype),
        compiler_params=_params("parallel", "arbitrary"),
        name="proj",
    )(x, w)


def _kvproj_kernel(x_ref, wk_ref, wv_ref, *refs, npt, heads, ct, has_prev):
    kv16_ref, kp_ref, vp_ref, ks_ref, vs_ref = refs[4:] if has_prev else refs
    is_p = pl.program_id(0) < npt
    tm = x_ref.shape[0]
    fw = heads * LANES

    def project(k_dst, v_dst):
        x = x_ref[...]
        for dst_ref, w_ref, c0 in ((k_dst, wk_ref, 0), (v_dst, wv_ref, fw)):
            for cj in range(fw // ct):
                r = _dot(x, w_ref[:, cj * ct:(cj + 1) * ct])
                kv16_ref[:, c0 + cj * ct:c0 + (cj + 1) * ct] = r.astype(kv16_ref.dtype)
                for hh in range(ct // LANES):
                    dst_ref[pl.ds(cj * (ct // LANES) + hh, tm, stride=heads), :] = (
                        r[:, hh * LANES:(hh + 1) * LANES])

    pl.when(is_p)(functools.partial(project, kp_ref, vp_ref))
    pl.when(jnp.logical_not(is_p))(functools.partial(project, ks_ref, vs_ref))


def _kvproj(x, w, prev, *, layer, col_k, col_v, depth, heads, np_rows, tm):
    t, d = x.shape
    fw = heads * LANES
    n = 2 * fw
    assert col_k % fw == 0 and col_v % fw == 0
    npt = np_rows // tm
    ns_rows = t - np_rows
    ct = 512 if fw % 512 == 0 else LANES
    blk = (None, tm * heads, LANES)
    p_spec = pl.BlockSpec(blk, lambda i: (layer, jnp.minimum(i, npt - 1), 0))
    s_spec = pl.BlockSpec(blk, lambda i: (layer, jnp.maximum(i - npt, 0), 0))
    p_shape = jax.ShapeDtypeStruct((depth, np_rows * heads, LANES), F32)
    s_shape = jax.ShapeDtypeStruct((depth, ns_rows * heads, LANES), F32)
    has_prev = prev is not None
    any_spec = pl.BlockSpec(memory_space=pl.ANY)
    return pl.pallas_call(
        functools.partial(_kvproj_kernel, npt=npt, heads=heads, ct=ct, has_prev=has_prev),
        grid=(t // tm,),
        in_specs=[pl.BlockSpec((tm, d), lambda i: (i, 0)),
                  pl.BlockSpec((None, d, fw), lambda i: (layer, 0, col_k // fw), pipeline_mode=pl.Buffered(1)),
                  pl.BlockSpec((None, d, fw), lambda i: (layer, 0, col_v // fw), pipeline_mode=pl.Buffered(1))]
        + ([any_spec] * 4 if has_prev else []),
        out_specs=[pl.BlockSpec((tm, n), lambda i: (i, 0)), p_spec, p_spec, s_spec, s_spec],
        out_shape=[jax.ShapeDtypeStruct((t, n), BF16), p_shape, p_shape, s_shape, s_shape],
        input_output_aliases={3: 1, 4: 2, 5: 3, 6: 4} if has_prev else {},
        compiler_params=_params("arbitrary"),
        name="kvproj",
    )(x, w, w, *(prev if has_prev else ()))


def _mm_logsig_kernel(x_ref, w_ref, b_ref, o_ref):
    y = _dot(x_ref[...], w_ref[...]) + b_ref[...]
    o_ref[...] = jnp.minimum(y, 0.0) - jnp.log(1.0 + jnp.exp(-jnp.abs(y)))


def _mm_logsig(x, w, b, *, tm):
    t, d = x.shape
    n = w.shape[1]
    return pl.pallas_call(
        _mm_logsig_kernel,
        grid=(t // tm,),
        in_specs=[pl.BlockSpec((tm, d), lambda i: (i, 0)),
                  pl.BlockSpec((d, n), lambda i: (0, 0)),
                  pl.BlockSpec((1, n), lambda i: (0, 0))],
        out_specs=pl.BlockSpec((tm, n), lambda i: (i, 0)),
        out_shape=jax.ShapeDtypeStruct((t, n), F32),
        compiler_params=_params("parallel"),
        name="fox_logf",
    )(x, w, b)


def _merge_kernel(o_ref, y_ref, wa_ref, wb_ref, ga_ref, gb_ref, out_ref):
    a = _dot(o_ref[...], wa_ref[...])
    b = _dot(y_ref[...], wb_ref[...])
    out_ref[...] = (_sigmoid(ga_ref[...]) * a + _sigmoid(gb_ref[...]) * b).astype(out_ref.dtype)


def _merge(o, y, wa, wb, layer, p32, col_ga, col_gb, *, tm, tn):
    t, hw = o.shape
    fw = y.shape[1]
    d = wa.shape[2]
    ja, jb = col_ga // tn, col_gb // tn
    return pl.pallas_call(
        _merge_kernel,
        grid=(t // tm, d // tn),
        in_specs=[pl.BlockSpec((tm, hw), lambda i, j: (i, 0)),
                  pl.BlockSpec((tm, fw), lambda i, j: (i, 0)),
                  pl.BlockSpec((None, hw, tn), lambda i, j: (layer, 0, j)),
                  pl.BlockSpec((None, fw, tn), lambda i, j: (layer, 0, j)),
                  pl.BlockSpec((tm, tn), lambda i, j: (i, ja + j)),
                  pl.BlockSpec((tm, tn), lambda i, j: (i, jb + j))],
        out_specs=pl.BlockSpec((tm, tn), lambda i, j: (i, j)),
        out_shape=jax.ShapeDtypeStruct((t, d), BF16),
        compiler_params=_params("parallel", "arbitrary"),
        name="merge",
    )(o, y, wa, wb, p32, p32)


def _outproj_kernel(m_ref, w_ref, x_ref, gp_ref, gs_ref, o_ref, *, npt, gt):
    r = _dot(m_ref[...], w_ref[...])
    tm, tn = r.shape
    gate = _pick(pl.program_id(0) < npt, gp_ref, gs_ref)
    y = gate * r.reshape(gt, tm // gt, tn)
    o_ref[...] = x_ref[...] + y.reshape(tm, tn)


def _outproj(m, w, layer, x, mod_p, mod_s, k_gate, *, tm, tn, npt, gt):
    t, d = x.shape
    gp, gs = _mod_specs(k_gate, npt, gt, tn, col=True)
    return pl.pallas_call(
        functools.partial(_outproj_kernel, npt=npt, gt=gt),
        grid=(t // tm, d // tn),
        in_specs=[pl.BlockSpec((tm, d), lambda i, j: (i, 0)),
                  pl.BlockSpec((None, d, tn), lambda i, j: (layer, 0, j)),
                  pl.BlockSpec((tm, tn), lambda i, j: (i, j)),
                  gp, gs],
        out_specs=pl.BlockSpec((tm, tn), lambda i, j: (i, j)),
        out_shape=jax.ShapeDtypeStruct((t, d), F32),
        compiler_params=_params("parallel", "arbitrary"),
        name="outproj",
    )(m, w, x, mod_p, mod_s)


def _per_head(fn, hb):
    return jnp.concatenate([fn(slice(h * LANES, (h + 1) * LANES)) for h in range(hb)], axis=1)


def _hgrn_heads(q16, z, v16, ag, sts, lg, hgn, *, layer):
    C, W = z.shape
    hb = W // LANES
    sub = min(HGRN_SUB, C)

    def head_sum(x):
        return _per_head(lambda ln: jnp.broadcast_to(
            jnp.sum(x[:, ln], axis=-1, keepdims=True), (x.shape[0], LANES)), hb)

    e = jnp.exp(lg - jnp.max(lg, axis=0, keepdims=True))
    soft = e / jnp.sum(e, axis=0, keepdims=True)
    lb = jnp.zeros((1, W), F32)
    for i in range(1, layer + 1):
        lb = lb + soft[i:i + 1]

    ez = jnp.exp(-jnp.abs(z))
    r = 1.0 / (1.0 + ez)
    sig_pos = jnp.where(z >= 0, r, ez * r)
    sig_neg = jnp.where(z >= 0, ez * r, r)
    g = jnp.log(lb + (1.0 - lb) * sig_pos) * LOG2E
    kin = (1.0 - lb) * sig_neg

    row = lax.broadcasted_iota(jnp.int32, (C, C), 0)
    col = lax.broadcasted_iota(jnp.int32, (C, C), 1)
    tri = (row >= col).astype(BF16)
    g1, g2, g3 = _split3(g)
    cum = _dot(tri, g1) + _dot(tri, g2) + _dot(tri, g3)
    cum_k = cum - jnp.log(kin) * LOG2E

    q = q16.astype(F32)
    v = v16.astype(F32)
    qe_all = (q * jnp.exp2(cum)).astype(BF16)
    sts16 = [st.astype(BF16) for st in sts]
    nsub = C // sub
    o_inter = _per_head(lambda ln: _dot_nt(qe_all[:, ln], sts16[ln.start // LANES]), hb)
    att = {}
    for i in range(1, nsub):
        r0 = i * sub
        b_i = cum[r0 - 1:r0]
        qe = (q[r0:r0 + sub] * jnp.exp2(cum[r0:r0 + sub] - b_i)).astype(BF16)
        ke = (kin[:r0] * jnp.exp2(b_i - cum[:r0])).astype(BF16)
        for h in range(hb):
            ln = slice(h * LANES, (h + 1) * LANES)
            att[i, h] = _dot_nt(qe[:, ln], ke[:, ln]).astype(BF16)
    o_off = {i: _per_head(lambda ln: _dot(att[i, ln.start // LANES], v16[:i * sub, ln]), hb)
             for i in range(1, nsub)}

    row8 = lax.broadcasted_iota(jnp.int32, (SUBLANES, 1), 0)
    outs = []
    for i in range(nsub):
        r0 = i * sub
        acc = o_inter[r0:r0 + sub]
        if i > 0:
            acc = acc + o_off[i]
        for u in range(sub // SUBLANES):
            t0 = r0 + u * SUBLANES
            cum_t, q_t = cum[t0:t0 + SUBLANES], q[t0:t0 + SUBLANES]
            acc_t = acc[u * SUBLANES:(u + 1) * SUBLANES]
            for s in range(r0, t0 + SUBLANES):
                d = cum_t - cum_k[s:s + 1]
                if s >= t0:
                    d = jnp.where(row8 >= s - t0, d, MASK_VALUE)
                acc_t = acc_t + head_sum(q_t * jnp.exp2(d)) * v[s:s + 1]
            outs.append(acc_t)
    o = jnp.concatenate(outs, axis=0)

    o = o * lax.rsqrt(head_sum(o * o) * (1.0 / LANES) + EPS) * hgn
    agf = ag.astype(F32)
    o = o * (agf * _sigmoid(agf))

    last = cum[C - 1:C]
    kdec = (kin * jnp.exp2(last - cum)).astype(BF16)
    decay = jnp.exp2(last)
    new_sts = [sts[h] * decay[:, h * LANES:(h + 1) * LANES]
               + _dot_tn(v16[:, h * LANES:(h + 1) * LANES], kdec[:, h * LANES:(h + 1) * LANES])
               for h in range(hb)]
    return o, new_sts


def _hgrn_kernel(q_ref, z_ref, v_ref, ag_ref, s0_ref, lbl_ref, hgn_ref, *refs, layer, nc, hb, has_prev):
    o_ref, sout_ref, st_ref = refs[1:] if has_prev else refs
    c = pl.program_id(2)

    @pl.when(c == 0)
    def _():
        for hh in range(hb):
            st_ref[hh] = s0_ref[hh].T

    o, new_sts = _hgrn_heads(q_ref[...], z_ref[...], v_ref[...], ag_ref[...],
                             [st_ref[hh] for hh in range(hb)], lbl_ref[...], hgn_ref[...], layer=layer)
    o_ref[...] = o.astype(o_ref.dtype)
    for hh in range(hb):
        st_ref[hh] = new_sts[hh]

    @pl.when(c == nc - 1)
    def _():
        for hh in range(hb):
            sout_ref[hh] = st_ref[hh].T


def _hgrn(p16, p32, s0, lb_logits, hg_norm, prev, *, layer, row0, nb, length, chunk, hb,
          col_q, col_v, col_z, col_ag):
    heads = s0.shape[1]
    nc = length // chunk
    rb0 = row0 // chunk
    depth = lb_logits.shape[0]
    w = hb * LANES
    has_prev = prev is not None

    def rows(b, h, c):
        return rb0 + b * nc + c

    return pl.pallas_call(
        functools.partial(_hgrn_kernel, layer=layer, nc=nc, hb=hb, has_prev=has_prev),
        grid=(nb, heads // hb, nc),
        in_specs=[pl.BlockSpec((chunk, w), lambda b, h, c: (rows(b, h, c), col_q // hb + h)),
                  pl.BlockSpec((chunk, w), lambda b, h, c: (rows(b, h, c), col_z // hb + h)),
                  pl.BlockSpec((chunk, w), lambda b, h, c: (rows(b, h, c), col_v // hb + h)),
                  pl.BlockSpec((chunk, w), lambda b, h, c: (rows(b, h, c), col_ag // hb + h)),
                  pl.BlockSpec((None, hb, LANES, LANES), lambda b, h, c: (b, h, 0, 0)),
                  pl.BlockSpec((depth, w), lambda b, h, c: (0, h)),
                  pl.BlockSpec((1, w), lambda b, h, c: (0, h))]
        + ([pl.BlockSpec(memory_space=pl.ANY)] if has_prev else []),
        out_specs=[pl.BlockSpec((chunk, w), lambda b, h, c: (rows(b, h, c), h)),
                   pl.BlockSpec((None, hb, LANES, LANES), lambda b, h, c: (b, h, 0, 0))],
        out_shape=[jax.ShapeDtypeStruct((p16.shape[0], heads * LANES), BF16),
                   jax.ShapeDtypeStruct((nb, heads, LANES, LANES), F32)],
        input_output_aliases={7: 0} if has_prev else {},
        scratch_shapes=[pltpu.VMEM((hb, LANES, LANES), F32)],
        compiler_params=_params("parallel", "parallel", "arbitrary"),
        name="hgrn",
    )(p16, p32, p16, p32, s0, lb_logits, hg_norm, *([prev] if has_prev else []))


def _cumsum_kernel(x_ref, o_ref, carry_ref):
    @pl.when(pl.program_id(0) == 0)
    def _():
        carry_ref[...] = jnp.zeros_like(carry_ref)

    x = x_ref[...]
    bw = x.shape[1]
    row = lax.broadcasted_iota(jnp.int32, (bw, bw), 0)
    col = lax.broadcasted_iota(jnp.int32, (bw, bw), 1)
    upper = (row <= col).astype(BF16)
    x1, x2, x3 = _split3(x)
    c = _dot(x1, upper) + _dot(x2, upper) + _dot(x3, upper) + carry_ref[...]
    o_ref[...] = c
    carry_ref[...] = c[:, bw - 1:bw]


def _cumsum_lanes(x):
    r, n = x.shape
    bw = 256 if n % 256 == 0 else LANES
    return pl.pallas_call(
        _cumsum_kernel,
        grid=(n // bw,),
        in_specs=[pl.BlockSpec((r, bw), lambda j: (0, j))],
        out_specs=pl.BlockSpec((r, bw), lambda j: (0, j)),
        out_shape=jax.ShapeDtypeStruct((r, n), F32),
        scratch_shapes=[pltpu.VMEM((r, 1), F32)],
        compiler_params=_params("arbitrary"),
        name="cumsum",
    )(x)


def _fox_update(s, v, m_ref, l_ref, acc_ref, slot, masked):
    tq, tk = s.shape
    if masked:
        row = lax.broadcasted_iota(jnp.int32, (tq, tk), 0)
        col = lax.broadcasted_iota(jnp.int32, (tq, tk), 1)
        s = jnp.where(row >= col, s, MASK_VALUE)
    m_old = m_ref[slot]
    m_new = jnp.maximum(m_old, jnp.max(s, axis=-1, keepdims=True))
    alpha = jnp.exp2(m_old - m_new)
    if tk % LANES == 0:
        ps = [jnp.exp2(s[:, c * LANES:(c + 1) * LANES] - m_new) for c in range(tk // LANES)]
        p_lanes = functools.reduce(jnp.add, ps)
        p = jnp.concatenate(ps, axis=1)
    else:
        p = jnp.exp2(s - m_new[:, :tk])
        p_lanes = jnp.concatenate([p, jnp.zeros((tq, LANES - tk), F32)], axis=1)
    l_ref[slot] = alpha * l_ref[slot] + p_lanes
    acc_ref[slot] = alpha * acc_ref[slot] + _dot(p.astype(BF16), v)
    m_ref[slot] = m_new


def _fox_init(m_ref, l_ref, acc_ref):
    m_ref[...] = jnp.full_like(m_ref, MASK_VALUE)
    l_ref[...] = jnp.zeros_like(l_ref)
    acc_ref[...] = jnp.zeros_like(acc_ref)


def _fox_out(l_ref, acc_ref, slot):
    return acc_ref[slot] / jnp.sum(l_ref[slot], axis=-1, keepdims=True)


def _fox_tile(q, k, v, bias, m_ref, l_ref, acc_ref, slot, masked):
    _fox_update(_dot_nt(q, k) + bias, v, m_ref, l_ref, acc_ref, slot, masked)


def _fox_prompt_kernel(q_ref, k_ref, v_ref, c_ref, o_ref, s_ref, m_ref, l_ref, acc_ref, kn_ref, *, tb, hp):
    qi = pl.program_id(1)
    nk = k_ref.shape[0] // tb
    chains = [(hh, half) for hh in range(hp) for half in range(2)]

    def lanes(hh):
        return slice(hh * LANES, (hh + 1) * LANES)

    @pl.when(qi == 0)
    def _():
        def kn_body(i, mx):
            kb = k_ref[pl.ds(pl.multiple_of(i * tb, tb), tb), :].astype(F32)
            for hh in range(hp):
                kh = kb[:, lanes(hh)]
                mx = jnp.maximum(mx, jnp.max(jnp.sum(kh * kh, axis=-1, keepdims=True), axis=0, keepdims=True))
            return mx
        kn_ref[...] = jnp.broadcast_to(lax.fori_loop(0, nk, kn_body, jnp.zeros((1, 1), F32)), kn_ref.shape)

    _fox_init(m_ref, l_ref, acc_ref)
    base = pl.multiple_of(qi * (2 * tb), 2 * tb)
    c_first = [c_ref[hh, :, pl.ds(base, LANES)][:, 0:1] for hh in range(hp)]
    qf = q_ref[...].astype(F32)
    qn2 = jnp.zeros((1, 1), F32)
    for hh in range(hp):
        qh = qf[:, lanes(hh)]
        qn2 = jnp.maximum(qn2, jnp.max(jnp.sum(qh * qh, axis=-1, keepdims=True), axis=0, keepdims=True))
    qk_max = jnp.sqrt(qn2 * kn_ref[:, 0:1])

    def scores(ch, slot, kstart):
        hh, half = chains[ch]
        bias = (c_first[hh] - c_ref[hh, :, pl.ds(kstart, tb)]) * LOG2E
        s_ref[ch, slot] = _dot_nt(q_ref[half * tb:(half + 1) * tb, lanes(hh)],
                                  k_ref[pl.ds(kstart, tb), lanes(hh)]) + bias

    def update(ch, slot, kstart, masked):
        hh, _ = chains[ch]
        _fox_update(s_ref[ch, slot], v_ref[pl.ds(kstart, tb), lanes(hh)], m_ref, l_ref, acc_ref, ch, masked)

    def tile_start(i):
        return pl.multiple_of(jnp.maximum(i, 0) * tb, tb)

    def live(kstart):
        bias_max = None
        for hh in range(hp):
            c_last = c_ref[hh, :, pl.ds(kstart + (tb - LANES), LANES)][:, LANES - 1:LANES]
            b = (c_first[hh] - c_last) * LOG2E
            bias_max = b if bias_max is None else jnp.maximum(bias_max, b)
        s_max = qk_max * (1.0 + 2.0 ** -6) + bias_max
        m_min = jnp.min(jnp.min(m_ref[...], axis=0), axis=0, keepdims=True)[:, 0:1]
        return jnp.where(s_max > m_min - EXP2_ZERO, 1, 0)[0, 0]

    top = pl.multiple_of(base + tb, tb)
    for ch, (hh, half) in enumerate(chains):
        if half == 1:
            scores(ch, 1, top)
        scores(ch, 0, base)
    for ch, (hh, half) in enumerate(chains):
        if half == 1:
            update(ch, 1, top, True)
        update(ch, 0, base, half == 0)

    n = 2 * qi
    for ch in range(len(chains)):
        scores(ch, 0, tile_start(n - 1))

    def body(carry):
        j, _ = carry
        k0, k1, k2 = tile_start(n - 1 - 2 * j), tile_start(n - 2 - 2 * j), tile_start(n - 3 - 2 * j)
        for ch in range(len(chains)):
            scores(ch, 1, k1)
            update(ch, 0, k0, False)
        for ch in range(len(chains)):
            scores(ch, 0, k2)
            update(ch, 1, k1, False)
        return j + 1, live(k2)

    lax.while_loop(lambda carry: jnp.logical_and(carry[0] < qi, carry[1] > 0), body,
                   (jnp.int32(0), live(tile_start(n - 1))))
    for ch, (hh, half) in enumerate(chains):
        o_ref[half * tb:(half + 1) * tb, lanes(hh)] = _fox_out(l_ref, acc_ref, ch).astype(o_ref.dtype)


def _fox_prompt(p16, kv16, c, *, heads, length, col_q, col_k, col_v, tb, hp):
    w = hp * LANES
    stat = pltpu.VMEM((2 * hp, tb, LANES), F32)
    return pl.pallas_call(
        functools.partial(_fox_prompt_kernel, tb=tb, hp=hp),
        grid=(heads // hp, length // (2 * tb)),
        in_specs=[pl.BlockSpec((2 * tb, w), lambda h, qi: (qi, col_q // hp + h)),
                  pl.BlockSpec((length, w), lambda h, qi: (0, col_k // hp + h)),
                  pl.BlockSpec((length, w), lambda h, qi: (0, col_v // hp + h)),
                  pl.BlockSpec((hp, 1, length), lambda h, qi: (h, 0, 0))],
        out_specs=pl.BlockSpec((2 * tb, w), lambda h, qi: (qi, h)),
        out_shape=jax.ShapeDtypeStruct((p16.shape[0], heads * LANES), BF16),
        scratch_shapes=[pltpu.VMEM((2 * hp, 2, tb, tb), F32), stat, stat, stat,
                        pltpu.VMEM((1, LANES), F32)],
        compiler_params=_params("parallel", "arbitrary"),
        name="fox_prompt",
    )(p16, kv16, kv16, c)


def _fox_sample_kernel(q_ref, kn_ref, vn_ref, kc_ref, vc_ref, cq_ref, ckc_ref, ckn_ref, ckl_ref, _, o_ref,
                       m_ref, l_ref, acc_ref, q1_ref, *, heads, nck):
    c = pl.program_id(1)
    tkc = kc_ref.shape[0] // heads
    cq_all = cq_ref[...] * LOG2E

    def head_lanes(h):
        return slice(h * LANES, (h + 1) * LANES)

    @pl.when(c == 0)
    def _():
        _fox_init(m_ref, l_ref, acc_ref)
        s_new = [_dot_nt(q_ref[:, head_lanes(h)], kn_ref[:, head_lanes(h)])
                 + (cq_all[:, h:h + 1] - ckn_ref[h] * LOG2E) for h in range(heads)]
        for h in range(heads):
            _fox_update(s_new[h], vn_ref[:, head_lanes(h)], m_ref, l_ref, acc_ref, h, True)
            q = q_ref[:, head_lanes(h)].astype(F32)
            q1 = jnp.max(jnp.sum(jnp.abs(q), axis=-1, keepdims=True), axis=0, keepdims=True)
            q1_ref[h] = jnp.broadcast_to(q1, (1, LANES))

    k_max = jnp.max(jnp.max(jnp.abs(kc_ref[...]), axis=0, keepdims=True), axis=-1, keepdims=True)
    cq_max = jnp.max(cq_all, axis=0, keepdims=True)
    c_last = ckl_ref[...] * LOG2E
    live = jnp.zeros((1, 1), jnp.int32)
    for h in range(heads):
        s_max = q1_ref[h][:, 0:1] * k_max * (1.0 + 2.0 ** -6) + (cq_max[:, h:h + 1] - c_last[:, h:h + 1])
        m_min = jnp.min(m_ref[h], axis=0, keepdims=True)[:, 0:1]
        live = jnp.maximum(live, jnp.where(s_max > m_min - EXP2_ZERO, 1, 0))

    @pl.when(live[0, 0] > 0)
    def _():
        s_all = []
        for h in range(heads):
            k = kc_ref[pl.ds(h, tkc, stride=heads), :].astype(BF16)
            s_all.append(_dot_nt(q_ref[:, head_lanes(h)], k) + (cq_all[:, h:h + 1] - ckc_ref[h] * LOG2E))
        for h in range(heads):
            v = vc_ref[pl.ds(h, tkc, stride=heads), :].astype(BF16)
            _fox_update(s_all[h], v, m_ref, l_ref, acc_ref, h, False)

    @pl.when(c == nck - 1)
    def _():
        for h in range(heads):
            o_ref[:, head_lanes(h)] = _fox_out(l_ref, acc_ref, h).astype(o_ref.dtype)


def _fox_sample(p16, kv16, cache_k, cache_v, cq, ckc, ckn, prev, *, layer, row0, nb, ls, heads,
                col_q, col_k, col_v, tkc):
    past = cache_k.shape[2] // heads
    nck = past // tkc
    rb0 = row0 // ls
    w = heads * LANES
    stat = pltpu.VMEM((heads, ls, LANES), F32)
    ckl = ckc[:, :, 0, tkc - 1::tkc].transpose(0, 2, 1)[:, :, None, :]

    def chunk(c):
        return nck - 1 - c

    return pl.pallas_call(
        functools.partial(_fox_sample_kernel, heads=heads, nck=nck),
        grid=(nb, nck),
        in_specs=[pl.BlockSpec((ls, w), lambda b, c: (rb0 + b, col_q // heads)),
                  pl.BlockSpec((ls, w), lambda b, c: (rb0 + b, col_k // heads)),
                  pl.BlockSpec((ls, w), lambda b, c: (rb0 + b, col_v // heads)),
                  pl.BlockSpec((None, None, tkc * heads, LANES), lambda b, c: (layer, b, chunk(c), 0)),
                  pl.BlockSpec((None, None, tkc * heads, LANES), lambda b, c: (layer, b, chunk(c), 0)),
                  pl.BlockSpec((None, ls, heads), lambda b, c: (b, 0, 0)),
                  pl.BlockSpec((None, heads, 1, tkc), lambda b, c: (b, 0, 0, chunk(c))),
                  pl.BlockSpec((None, heads, 1, ls), lambda b, c: (b, 0, 0, 0)),
                  pl.BlockSpec((None, None, 1, heads), lambda b, c: (b, chunk(c), 0, 0)),
                  pl.BlockSpec(memory_space=pl.ANY)],
        out_specs=pl.BlockSpec((ls, w), lambda b, c: (rb0 + b, 0)),
        out_shape=jax.ShapeDtypeStruct(prev.shape, BF16),
        input_output_aliases={9: 0},
        scratch_shapes=[stat, stat, stat, pltpu.VMEM((heads, 1, LANES), F32)],
        compiler_params=_params("parallel", "arbitrary"),
        name="fox_sample",
    )(p16, kv16, kv16, cache_k, cache_v, cq, ckc, ckn, ckl, prev)


def _row_tile(np_rows, ns_rows, ls):
    for tm in (512, 256, 128, 64, 32):
        if np_rows % tm == 0 and ns_rows % tm == 0 and tm % ls == 0:
            return tm
    raise ValueError("no row tile fits the prompt / sample row counts")


def kernel(x_prompt, x_sample, c_prompt, c_sample, cache_fox_k, cache_fox_v, cache_fox_logf,
           state_hgrn, ada_w, ada_b, ln_ffn1, ln_mix, ln_ffn2, ffn1_w_gate, ffn1_w_up,
           ffn1_w_down, ffn2_w_gate, ffn2_w_up, ffn2_w_down, w_in, hgrn_lb_logits,
           hgrn_norm_g, fox_f_bias, w_branch_a, w_branch_b, w_out, ln_final):
    bp, lp, d = x_prompt.shape
    bs, ls, _ = x_sample.shape
    depth = ada_w.shape[0]
    _, _, past, fh, fdh = cache_fox_k.shape
    _, _, hh, dk, dv = state_hgrn.shape
    assert bp == 1 and dk == LANES and dv == LANES and fdh == LANES
    assert ls % SUBLANES == 0 and d % LANES == 0
    np_rows, ns_rows = bp * lp, bs * ls
    t = np_rows + ns_rows
    hw, fw = hh * LANES, fh * LANES
    tm = _row_tile(np_rows, ns_rows, ls)
    npt, gt = np_rows // tm, tm // ls
    tn = 512 if (d % 512 == 0 and hw % 512 == 0 and fw % 512 == 0) else LANES
    chunk_p = 64 if lp % 64 == 0 else lp
    tb = 256 if lp % 512 == 0 else lp // 2
    tkc = 512 if past % 512 == 0 else past
    hb = next(n for n in (8, 4, 2, 1) if hh % n == 0)
    tm_kv = min(tm, 256)
    tm_big = 2 * tm if (np_rows % (2 * tm) == 0 and ns_rows % (2 * tm) == 0) else tm
    tiles = dict(tm=tm, npt=npt, gt=gt)
    tiles_big = dict(tm=tm_big, npt=np_rows // tm_big, gt=tm_big // ls)
    q_scale = LOG2E * float(fdh) ** -0.5

    x = jnp.concatenate([x_prompt.reshape(np_rows, d), x_sample.reshape(ns_rows, d)], axis=0)
    c_all = jnp.concatenate([c_prompt, c_sample], axis=0)
    m_pad = -(-c_all.shape[0] // SUBLANES) * SUBLANES
    c_pad = jnp.pad(c_all, ((0, m_pad - c_all.shape[0]), (0, 0)))
    s0_prompt = jnp.zeros((bp, hh, dk, dv), F32)

    sizes = (hw, hw, hw, hw, fw, fw, fw, fh, d, d)
    offs = [0]
    for s in sizes:
        offs.append(offs[-1] + s)
    o_aq, o_af, o_ai, o_ag, o_fq, o_fk, o_fv, o_ffl, o_ga, o_gb = offs[:10]

    outs = {k: [] for k in ("lfp", "sp", "lfs", "ss")}
    cache_k = cache_fox_k.reshape(depth, bs, past * fh, fdh)
    cache_v = cache_fox_v.reshape(depth, bs, past * fh, fdh)
    kv_out = None
    f1g, f1u, f1d = ffn1_w_gate.astype(BF16), ffn1_w_up.astype(BF16), ffn1_w_down.astype(BF16)
    f2g, f2u, f2d = ffn2_w_gate.astype(BF16), ffn2_w_up.astype(BF16), ffn2_w_down.astype(BF16)
    wba, wbb, wo = w_branch_a.astype(BF16), w_branch_b.astype(BF16), w_out.astype(BF16)
    wib = w_in.astype(BF16)
    wgate = wib[:, :, o_ga:o_ga + 2 * d]
    h = x
    for l in range(depth):
        wffl = jnp.pad(wib[l, :, o_ffl:o_ffl + fh], ((0, 0), (0, LANES - fh)))
        bffl = jnp.pad(fox_f_bias[l], (0, LANES - fh)).reshape(1, LANES)

        mod = _ada(c_pad, ada_w, ada_b, l)
        mod_p = mod[0:bp].reshape(bp, N_MOD, 1, d)
        mod_s = mod[bp:bp + bs].reshape(bs, N_MOD, 1, d)

        n1 = _normmod(h, ln_ffn1[l], mod_p, mod_s, 0, 1, **tiles)
        h = _ffn(n1, h, f1g, f1u, f1d, l, mod_p, mod_s, 2, **tiles)

        n2 = _normmod(h, ln_mix[l], mod_p, mod_s, 3, 4, **tiles)
        p16 = _mm(n2, wib, BF16, [(o_aq, hw), (o_ai, hw), (o_fq, fw)], layer=l, tm=tm_big, tn=tn,
                  scale_group=2, scale=q_scale)
        p32 = _mm(n2, wib, F32, [(o_af, hw), (o_ag, hw)], layer=l, tm=tm_big, tn=tn)
        pg = _mm(n2, wgate, F32, [(0, 2 * d)], layer=l, tm=tm_big, tn=tn)
        kv16, *kv_out = _kvproj(n2, wib, kv_out, layer=l, col_k=o_fk, col_v=o_fv, depth=depth,
                                heads=fh, np_rows=np_rows, tm=tm_kv)
        lf_pad = _mm_logsig(n2, wffl, bffl, tm=tm)

        cb = lambda off: off // LANES
        hg_cols = dict(col_q=0, col_v=cb(hw), col_z=0, col_ag=cb(hw), hb=hb)
        o_all, s_p = _hgrn(p16, p32, s0_prompt, hgrn_lb_logits, hgrn_norm_g[l].reshape(1, hw), None,
                           layer=l, row0=0, nb=bp, length=lp, chunk=chunk_p, **hg_cols)
        o_all, s_s = _hgrn(p16, p32, state_hgrn[l], hgrn_lb_logits, hgrn_norm_g[l].reshape(1, hw), o_all,
                           layer=l, row0=np_rows, nb=bs, length=ls, chunk=ls, **hg_cols)

        lf = lf_pad[:, :fh]
        lf_p, lf_s = lf[:np_rows], lf[np_rows:]
        c_p = _cumsum_lanes(lf_p.T)
        cache_t = cache_fox_logf[l].transpose(0, 2, 1).reshape(bs * fh, past)
        new_t = lf_s.reshape(bs, ls, fh).transpose(0, 2, 1).reshape(bs * fh, ls)
        n_all = -(-(past + ls) // LANES) * LANES
        c_s = _cumsum_lanes(jnp.concatenate(
            [cache_t, new_t, jnp.zeros((bs * fh, n_all - past - ls), F32)], axis=1))
        cq_s = c_s[:, past:past + ls].reshape(bs, fh, ls).transpose(0, 2, 1)
        ckc_s = c_s[:, :past].reshape(bs, fh, 1, past)
        ckn_s = c_s[:, past:past + ls].reshape(bs, fh, 1, ls)

        fox_cols = dict(col_q=cb(2 * hw), col_k=0, col_v=cb(fw))
        y_all = _fox_prompt(p16, kv16, c_p.reshape(fh, 1, lp), heads=fh, length=lp, tb=tb,
                            hp=2 if fh % 2 == 0 else 1, **fox_cols)
        y_all = _fox_sample(p16, kv16, cache_k, cache_v, cq_s, ckc_s, ckn_s, y_all, layer=l,
                            row0=np_rows, nb=bs, ls=ls, heads=fh, tkc=tkc, **fox_cols)

        merged = _merge(o_all, y_all, wba, wbb, l, pg, 0, d, tm=tm_big, tn=tn)
        h = _outproj(merged, wo, l, h, mod_p, mod_s, 5, tn=tn, **tiles_big)

        n3 = _normmod(h, ln_ffn2[l], mod_p, mod_s, 6, 7, **tiles)
        h = _ffn(n3, h, f2g, f2u, f2d, l, mod_p, mod_s, 8, **tiles)

        outs["lfp"].append(lf_p.reshape(bp, lp, fh))
        outs["sp"].append(s_p)
        outs["lfs"].append(lf_s.reshape(bs, ls, fh))
        outs["ss"].append(s_s)

    y_prompt = _final_norm(h, ln_final, row0=0, rows=np_rows, tm=tm).reshape(bp, lp, d)
    y_sample = _final_norm(h, ln_final, row0=np_rows, rows=ns_rows, tm=tm).reshape(bs, ls, d)
    st = {k: jnp.stack(v) for k, v in outs.items()}
    kp, vp, ks, vs = kv_out
    return (y_prompt, y_sample, kp.reshape(depth, bp, lp, fh, fdh), vp.reshape(depth, bp, lp, fh, fdh),
            st["lfp"], st["sp"], ks.reshape(depth, bs, ls, fh, fdh), vs.reshape(depth, bs, ls, fh, fdh),
            st["lfs"], st["ss"])
```

```python
import functools

import jax
import jax.numpy as jnp
from jax import lax
from jax.experimental import pallas as pl
from jax.experimental.pallas import tpu as pltpu

EPS = 1e-6
MASK_VALUE = -1e30
N_MOD = 9
LANES = 128
SUBLANES = 8
V7X_VMEM_BYTES = 64 * 1024 * 1024
VMEM_LIMIT = V7X_VMEM_BYTES * 3 // 4
HGRN_SUB = 16
LOG2E = 1.4426950408889634
EXP2_ZERO = 152.0

F32 = jnp.float32
BF16 = jnp.bfloat16


def _params(*sem):
    return pltpu.CompilerParams(dimension_semantics=sem, vmem_limit_bytes=VMEM_LIMIT)


def _sigmoid(x):
    return 1.0 / (1.0 + jnp.exp(-x))


def _dot(a, b):
    return jnp.dot(a, b, preferred_element_type=F32)


def _dot_nt(a, b):
    return lax.dot_general(a, b, (((1,), (1,)), ((), ())), preferred_element_type=F32)


def _dot_tn(a, b):
    return lax.dot_general(a, b, (((0,), (0,)), ((), ())), preferred_element_type=F32)


def _split3(x):
    x1 = x.astype(BF16)
    r1 = x - x1.astype(F32)
    x2 = r1.astype(BF16)
    x3 = (r1 - x2.astype(F32)).astype(BF16)
    return x1, x2, x3


def _pick(is_prompt, p_ref, s_ref):
    return jnp.where(is_prompt, p_ref[...][None], s_ref[...])


def _mod_specs(k, npt, gt, n, col=None):
    if col is None:
        p = pl.BlockSpec((None, None, 1, n), lambda i, *_: (0, k, 0, 0))
        s = pl.BlockSpec((gt, None, 1, n), lambda i, *_: (jnp.maximum(i - npt, 0), k, 0, 0))
    else:
        p = pl.BlockSpec((None, None, 1, n), lambda i, j: (0, k, 0, j))
        s = pl.BlockSpec((gt, None, 1, n), lambda i, j: (jnp.maximum(i - npt, 0), k, 0, j))
    return p, s


def _ada_kernel(c_ref, w_ref, b_ref, o_ref):
    c = c_ref[...]
    a = (c * _sigmoid(c)).astype(BF16)
    o_ref[...] = _dot(a, w_ref[...].astype(BF16)) + b_ref[...]


def _ada(c_pad, w, b, layer):
    m, d = c_pad.shape
    depth, _, n = w.shape
    tn = 1024 if n % 1024 == 0 else n
    return pl.pallas_call(
        _ada_kernel,
        grid=(n // tn,),
        in_specs=[pl.BlockSpec((m, d), lambda j: (0, 0)),
                  pl.BlockSpec((d, tn), lambda j: (layer, j)),
                  pl.BlockSpec((None, 1, tn), lambda j: (layer, 0, j))],
        out_specs=pl.BlockSpec((m, tn), lambda j: (0, j)),
        out_shape=jax.ShapeDtypeStruct((m, n), F32),
        compiler_params=_params("arbitrary"),
        name="ada",
    )(c_pad, w.reshape(depth * d, n), b.reshape(depth, 1, n))


def _normmod_kernel(x_ref, g_ref, shp_ref, shs_ref, scp_ref, scs_ref, o_ref, *, npt, gt):
    is_p = pl.program_id(0) < npt
    x = x_ref[...]
    tm, d = x.shape
    y = x * lax.rsqrt(jnp.mean(x * x, axis=-1, keepdims=True) + EPS) * g_ref[...]
    sh = _pick(is_p, shp_ref, shs_ref)
    sc = _pick(is_p, scp_ref, scs_ref)
    y = y.reshape(gt, tm // gt, d) * (1.0 + sc) + sh
    o_ref[...] = y.reshape(tm, d).astype(o_ref.dtype)


def _normmod(x, g, mod_p, mod_s, k_shift, k_scale, *, tm, npt, gt):
    t, d = x.shape
    shp, shs = _mod_specs(k_shift, npt, gt, d)
    scp, scs = _mod_specs(k_scale, npt, gt, d)
    return pl.pallas_call(
        functools.partial(_normmod_kernel, npt=npt, gt=gt),
        grid=(t // tm,),
        in_specs=[pl.BlockSpec((tm, d), lambda i: (i, 0)),
                  pl.BlockSpec((1, d), lambda i: (0, 0)),
                  shp, shs, scp, scs],
        out_specs=pl.BlockSpec((tm, d), lambda i: (i, 0)),
        out_shape=jax.ShapeDtypeStruct((t, d), BF16),
        compiler_params=_params("parallel"),
        name="normmod",
    )(x, g.reshape(1, d), mod_p, mod_s, mod_p, mod_s)


def _rmsnorm_kernel(x_ref, g_ref, o_ref):
    x = x_ref[...]
    o_ref[...] = x * lax.rsqrt(jnp.mean(x * x, axis=-1, keepdims=True) + EPS) * g_ref[...]


def _final_norm(x, g, *, row0, rows, tm):
    d = x.shape[1]
    b0 = row0 // tm
    return pl.pallas_call(
        _rmsnorm_kernel,
        grid=(rows // tm,),
        in_specs=[pl.BlockSpec((tm, d), lambda i: (b0 + i, 0)),
                  pl.BlockSpec((1, d), lambda i: (0, 0))],
        out_specs=pl.BlockSpec((tm, d), lambda i: (i, 0)),
        out_shape=jax.ShapeDtypeStruct((rows, d), F32),
        compiler_params=_params("parallel"),
        name="final_norm",
    )(x, g.reshape(1, d))


def _ffn_kernel(x_ref, g_ref, shp_ref, shs_ref, scp_ref, scs_ref, wg_ref, wu_ref, wd_ref, gp_ref, gs_ref,
                o_ref, n_ref, *, npt, gt):
    i, j = pl.program_id(0), pl.program_id(1)
    is_p = i < npt
    tm, d = o_ref.shape

    @pl.when(j == 0)
    def _():
        x = x_ref[...]
        y = x * lax.rsqrt(jnp.mean(x * x, axis=-1, keepdims=True) + EPS) * g_ref[...]
        y = y.reshape(gt, tm // gt, d) * (1.0 + _pick(is_p, scp_ref, scs_ref)) + _pick(is_p, shp_ref, shs_ref)
        n_ref[...] = y.reshape(tm, d).astype(n_ref.dtype)
        o_ref[...] = jnp.zeros_like(o_ref)

    n = n_ref[...]
    a = _dot(n, wg_ref[...])
    b = _dot(n, wu_ref[...])
    o_ref[...] += _dot((a * _sigmoid(a) * b).astype(BF16), wd_ref[...])

    @pl.when(j == pl.num_programs(1) - 1)
    def _():
        gate = _pick(is_p, gp_ref, gs_ref)
        y = (0.5 * gate) * o_ref[...].reshape(gt, tm // gt, d)
        o_ref[...] = x_ref[...] + y.reshape(tm, d)


def _ffn(x, ln_g, wg, wu, wd, layer, mod_p, mod_s, k_shift, k_scale, k_gate, *, tm, npt, gt):
    t, d = x.shape
    f = wg.shape[2]
    tf = 512 if f % 512 == 0 else f
    shp, shs = _mod_specs(k_shift, npt, gt, d)
    scp, scs = _mod_specs(k_scale, npt, gt, d)
    gp, gs = _mod_specs(k_gate, npt, gt, d)
    return pl.pallas_call(
        functools.partial(_ffn_kernel, npt=npt, gt=gt),
        grid=(t // tm, f // tf),
        in_specs=[pl.BlockSpec((tm, d), lambda i, j: (i, 0)),
                  pl.BlockSpec((1, d), lambda i, j: (0, 0)),
                  shp, shs, scp, scs,
                  pl.BlockSpec((None, d, tf), lambda i, j: (layer, 0, j)),
                  pl.BlockSpec((None, d, tf), lambda i, j: (layer, 0, j)),
                  pl.BlockSpec((None, tf, d), lambda i, j: (layer, j, 0)),
                  gp, gs],
        out_specs=pl.BlockSpec((tm, d), lambda i, j: (i, 0)),
        out_shape=jax.ShapeDtypeStruct((t, d), F32),
        scratch_shapes=[pltpu.VMEM((tm, d), BF16)],
        compiler_params=_params("parallel", "arbitrary"),
        name="ffn",
    )(x, ln_g.reshape(1, d), mod_p, mod_s, mod_p, mod_s, wg, wu, wd, mod_p, mod_s)


def _mm_kernel(x_ref, w_ref, o_ref, *, scale_from, scale):
    r = _dot(x_ref[...], w_ref[...])
    if scale_from is not None:
        r = r * jnp.where(pl.program_id(1) >= scale_from, scale, 1.0)
    o_ref[...] = r.astype(o_ref.dtype)


def _mm(x, w, out_dtype, groups, *, layer, tm, tn, scale_group=None, scale=1.0):
    t, d = x.shape
    starts = [s // tn for s, _ in groups]
    widths = [w_ // tn for _, w_ in groups]
    assert all(s % tn == 0 and w_ % tn == 0 for s, w_ in groups)
    out_starts = [sum(widths[:g]) for g in range(len(groups))]
    n = sum(widths) * tn

    def wcol(j):
        col = j + (starts[0] - out_starts[0])
        for g in range(1, len(groups)):
            step = (starts[g] - out_starts[g]) - (starts[g - 1] - out_starts[g - 1])
            col = col + jnp.where(j >= out_starts[g], step, 0)
        return col

    assert scale_group is None or scale_group == len(groups) - 1
    scale_from = None if scale_group is None else out_starts[scale_group]
    return pl.pallas_call(
        functools.partial(_mm_kernel, scale_from=scale_from, scale=scale),
        grid=(t // tm, n // tn),
        in_specs=[pl.BlockSpec((tm, d), lambda i, j: (i, 0)),
                  pl.BlockSpec((None, d, tn), lambda i, j: (layer, 0, wcol(j)))],
        out_specs=pl.BlockSpec((tm, tn), lambda i, j: (i, j)),
        out_shape=jax.ShapeDtypeStruct((t, n), out_dtype),
        compiler_params=_params("parallel", "arbitrary"),
        name="proj",
    )(x, w)


def _kvproj_kernel(x_ref, wk_ref, wv_ref, *refs, npt, heads, ct, has_prev):
    kv16_ref, kp_ref, vp_ref, ks_ref, vs_ref = refs[4:] if has_prev else refs
    is_p = pl.program_id(0) < npt
    tm = x_ref.shape[0]
    fw = heads * LANES

    def project(k_dst, v_dst):
        x = x_ref[...]
        for dst_ref, w_ref, c0 in ((k_dst, wk_ref, 0), (v_dst, wv_ref, fw)):
            for cj in range(fw // ct):
                r = _dot(x, w_ref[:, cj * ct:(cj + 1) * ct])
                kv16_ref[:, c0 + cj * ct:c0 + (cj + 1) * ct] = r.astype(kv16_ref.dtype)
                for hh in range(ct // LANES):
                    dst_ref[pl.ds(cj * (ct // LANES) + hh, tm, stride=heads), :] = (
                        r[:, hh * LANES:(hh + 1) * LANES])

    pl.when(is_p)(functools.partial(project, kp_ref, vp_ref))
    pl.when(jnp.logical_not(is_p))(functools.partial(project, ks_ref, vs_ref))


def _kvproj(x, w, prev, *, layer, col_k, col_v, depth, heads, np_rows, tm):
    t, d = x.shape
    fw = heads * LANES
    n = 2 * fw
    assert col_k % fw == 0 and col_v % fw == 0
    npt = np_rows // tm
    ns_rows = t - np_rows
    ct = 512 if fw % 512 == 0 else LANES
    blk = (None, tm * heads, LANES)
    p_spec = pl.BlockSpec(blk, lambda i: (layer, jnp.minimum(i, npt - 1), 0))
    s_spec = pl.BlockSpec(blk, lambda i: (layer, jnp.maximum(i - npt, 0), 0))
    p_shape = jax.ShapeDtypeStruct((depth, np_rows * heads, LANES), F32)
    s_shape = jax.ShapeDtypeStruct((depth, ns_rows * heads, LANES), F32)
    has_prev = prev is not None
    any_spec = pl.BlockSpec(memory_space=pl.ANY)
    return pl.pallas_call(
        functools.partial(_kvproj_kernel, npt=npt, heads=heads, ct=ct, has_prev=has_prev),
        grid=(t // tm,),
        in_specs=[pl.BlockSpec((tm, d), lambda i: (i, 0)),
                  pl.BlockSpec((None, d, fw), lambda i: (layer, 0, col_k // fw), pipeline_mode=pl.Buffered(1)),
                  pl.BlockSpec((None, d, fw), lambda i: (layer, 0, col_v // fw), pipeline_mode=pl.Buffered(1))]
        + ([any_spec] * 4 if has_prev else []),
        out_specs=[pl.BlockSpec((tm, n), lambda i: (i, 0)), p_spec, p_spec, s_spec, s_spec],
        out_shape=[jax.ShapeDtypeStruct((t, n), BF16), p_shape, p_shape, s_shape, s_shape],
        input_output_aliases={3: 1, 4: 2, 5: 3, 6: 4} if has_prev else {},
        compiler_params=_params("arbitrary"),
        name="kvproj",
    )(x, w, w, *(prev if has_prev else ()))


def _mm_logsig_kernel(x_ref, w_ref, b_ref, o_ref):
    y = _dot(x_ref[...], w_ref[...]) + b_ref[...]
    o_ref[...] = jnp.minimum(y, 0.0) - jnp.log(1.0 + jnp.exp(-jnp.abs(y)))


def _mm_logsig(x, w, b, *, tm):
    t, d = x.shape
    n = w.shape[1]
    return pl.pallas_call(
        _mm_logsig_kernel,
        grid=(t // tm,),
        in_specs=[pl.BlockSpec((tm, d), lambda i: (i, 0)),
                  pl.BlockSpec((d, n), lambda i: (0, 0)),
                  pl.BlockSpec((1, n), lambda i: (0, 0))],
        out_specs=pl.BlockSpec((tm, n), lambda i: (i, 0)),
        out_shape=jax.ShapeDtypeStruct((t, n), F32),
        compiler_params=_params("parallel"),
        name="fox_logf",
    )(x, w, b)


def _merge_kernel(o_ref, y_ref, wa_ref, wb_ref, ga_ref, gb_ref, out_ref):
    a = _dot(o_ref[...], wa_ref[...])
    b = _dot(y_ref[...], wb_ref[...])
    out_ref[...] = (_sigmoid(ga_ref[...]) * a + _sigmoid(gb_ref[...]) * b).astype(out_ref.dtype)


def _merge(o, y, wa, wb, layer, p32, col_ga, col_gb, *, tm, tn):
    t, hw = o.shape
    fw = y.shape[1]
    d = wa.shape[2]
    ja, jb = col_ga // tn, col_gb // tn
    return pl.pallas_call(
        _merge_kernel,
        grid=(t // tm, d // tn),
        in_specs=[pl.BlockSpec((tm, hw), lambda i, j: (i, 0)),
                  pl.BlockSpec((tm, fw), lambda i, j: (i, 0)),
                  pl.BlockSpec((None, hw, tn), lambda i, j: (layer, 0, j)),
                  pl.BlockSpec((None, fw, tn), lambda i, j: (layer, 0, j)),
                  pl.BlockSpec((tm, tn), lambda i, j: (i, ja + j)),
                  pl.BlockSpec((tm, tn), lambda i, j: (i, jb + j))],
        out_specs=pl.BlockSpec((tm, tn), lambda i, j: (i, j)),
        out_shape=jax.ShapeDtypeStruct((t, d), BF16),
        compiler_params=_params("parallel", "arbitrary"),
        name="merge",
    )(o, y, wa, wb, p32, p32)


def _outproj_kernel(m_ref, w_ref, x_ref, gp_ref, gs_ref, o_ref, *, npt, gt):
    r = _dot(m_ref[...], w_ref[...])
    tm, tn = r.shape
    gate = _pick(pl.program_id(0) < npt, gp_ref, gs_ref)
    y = gate * r.reshape(gt, tm // gt, tn)
    o_ref[...] = x_ref[...] + y.reshape(tm, tn)


def _outproj(m, w, layer, x, mod_p, mod_s, k_gate, *, tm, tn, npt, gt):
    t, d = x.shape
    gp, gs = _mod_specs(k_gate, npt, gt, tn, col=True)
    return pl.pallas_call(
        functools.partial(_outproj_kernel, npt=npt, gt=gt),
        grid=(t // tm, d // tn),
        in_specs=[pl.BlockSpec((tm, d), lambda i, j: (i, 0)),
                  pl.BlockSpec((None, d, tn), lambda i, j: (layer, 0, j)),
                  pl.BlockSpec((tm, tn), lambda i, j: (i, j)),
                  gp, gs],
        out_specs=pl.BlockSpec((tm, tn), lambda i, j: (i, j)),
        out_shape=jax.ShapeDtypeStruct((t, d), F32),
        compiler_params=_params("parallel", "arbitrary"),
        name="outproj",
    )(m, w, x, mod_p, mod_s)


def _per_head(fn, hb):
    return jnp.concatenate([fn(slice(h * LANES, (h + 1) * LANES)) for h in range(hb)], axis=1)


def _hgrn_heads(q16, z, v16, ag, sts, lg, hgn, *, layer):
    C, W = z.shape
    hb = W // LANES
    sub = min(HGRN_SUB, C)

    def head_sum(x):
        return _per_head(lambda ln: jnp.broadcast_to(
            jnp.sum(x[:, ln], axis=-1, keepdims=True), (x.shape[0], LANES)), hb)

    e = jnp.exp(lg - jnp.max(lg, axis=0, keepdims=True))
    soft = e / jnp.sum(e, axis=0, keepdims=True)
    lb = jnp.zeros((1, W), F32)
    for i in range(1, layer + 1):
        lb = lb + soft[i:i + 1]

    ez = jnp.exp(-jnp.abs(z))
    r = 1.0 / (1.0 + ez)
    sig_pos = jnp.where(z >= 0, r, ez * r)
    sig_neg = jnp.where(z >= 0, ez * r, r)
    g = jnp.log(lb + (1.0 - lb) * sig_pos) * LOG2E
    kin = (1.0 - lb) * sig_neg

    row = lax.broadcasted_iota(jnp.int32, (C, C), 0)
    col = lax.broadcasted_iota(jnp.int32, (C, C), 1)
    tri = (row >= col).astype(BF16)
    g1, g2, g3 = _split3(g)
    cum = _dot(tri, g1) + _dot(tri, g2) + _dot(tri, g3)
    cum_k = cum - jnp.log(kin) * LOG2E

    q = q16.astype(F32)
    v = v16.astype(F32)
    qe_all = (q * jnp.exp2(cum)).astype(BF16)
    sts16 = [st.astype(BF16) for st in sts]
    nsub = C // sub
    o_inter = _per_head(lambda ln: _dot_nt(qe_all[:, ln], sts16[ln.start // LANES]), hb)
    att = {}
    for i in range(1, nsub):
        r0 = i * sub
        b_i = cum[r0 - 1:r0]
        qe = (q[r0:r0 + sub] * jnp.exp2(cum[r0:r0 + sub] - b_i)).astype(BF16)
        ke = (kin[:r0] * jnp.exp2(b_i - cum[:r0])).astype(BF16)
        for h in range(hb):
            ln = slice(h * LANES, (h + 1) * LANES)
            att[i, h] = _dot_nt(qe[:, ln], ke[:, ln]).astype(BF16)
    o_off = {i: _per_head(lambda ln: _dot(att[i, ln.start // LANES], v16[:i * sub, ln]), hb)
             for i in range(1, nsub)}

    row8 = lax.broadcasted_iota(jnp.int32, (SUBLANES, 1), 0)
    outs = []
    for i in range(nsub):
        r0 = i * sub
        acc = o_inter[r0:r0 + sub]
        if i > 0:
            acc = acc + o_off[i]
        for u in range(sub // SUBLANES):
            t0 = r0 + u * SUBLANES
            cum_t, q_t = cum[t0:t0 + SUBLANES], q[t0:t0 + SUBLANES]
            acc_t = acc[u * SUBLANES:(u + 1) * SUBLANES]
            for s in range(r0, t0 + SUBLANES):
                d = cum_t - cum_k[s:s + 1]
                if s >= t0:
                    d = jnp.where(row8 >= s - t0, d, MASK_VALUE)
                acc_t = acc_t + head_sum(q_t * jnp.exp2(d)) * v[s:s + 1]
            outs.append(acc_t)
    o = jnp.concatenate(outs, axis=0)

    o = o * lax.rsqrt(head_sum(o * o) * (1.0 / LANES) + EPS) * hgn
    agf = ag.astype(F32)
    o = o * (agf * _sigmoid(agf))

    last = cum[C - 1:C]
    kdec = (kin * jnp.exp2(last - cum)).astype(BF16)
    decay = jnp.exp2(last)
    new_sts = [sts[h] * decay[:, h * LANES:(h + 1) * LANES]
               + _dot_tn(v16[:, h * LANES:(h + 1) * LANES], kdec[:, h * LANES:(h + 1) * LANES])
               for h in range(hb)]
    return o, new_sts


def _hgrn_kernel(q_ref, z_ref, v_ref, ag_ref, s0_ref, lbl_ref, hgn_ref, *refs, layer, nc, hb, has_prev):
    o_ref, sout_ref, st_ref = refs[1:] if has_prev else refs
    c = pl.program_id(2)

    @pl.when(c == 0)
    def _():
        for hh in range(hb):
            st_ref[hh] = s0_ref[hh].T

    o, new_sts = _hgrn_heads(q_ref[...], z_ref[...], v_ref[...], ag_ref[...],
                             [st_ref[hh] for hh in range(hb)], lbl_ref[...], hgn_ref[...], layer=layer)
    o_ref[...] = o.astype(o_ref.dtype)
    for hh in range(hb):
        st_ref[hh] = new_sts[hh]

    @pl.when(c == nc - 1)
    def _():
        for hh in range(hb):
            sout_ref[hh] = st_ref[hh].T


def _hgrn(p16, p32, s0, lb_logits, hg_norm, prev, *, layer, row0, nb, length, chunk, hb,
          col_q, col_v, col_z, col_ag):
    heads = s0.shape[1]
    nc = length // chunk
    rb0 = row0 // chunk
    depth = lb_logits.shape[0]
    w = hb * LANES
    has_prev = prev is not None

    def rows(b, h, c):
        return rb0 + b * nc + c

    return pl.pallas_call(
        functools.partial(_hgrn_kernel, layer=layer, nc=nc, hb=hb, has_prev=has_prev),
        grid=(nb, heads // hb, nc),
        in_specs=[pl.BlockSpec((chunk, w), lambda b, h, c: (rows(b, h, c), col_q // hb + h)),
                  pl.BlockSpec((chunk, w), lambda b, h, c: (rows(b, h, c), col_z // hb + h)),
                  pl.BlockSpec((chunk, w), lambda b, h, c: (rows(b, h, c), col_v // hb + h)),
                  pl.BlockSpec((chunk, w), lambda b, h, c: (rows(b, h, c), col_ag // hb + h)),
                  pl.BlockSpec((None, hb, LANES, LANES), lambda b, h, c: (b, h, 0, 0)),
                  pl.BlockSpec((depth, w), lambda b, h, c: (0, h)),
                  pl.BlockSpec((1, w), lambda b, h, c: (0, h))]
        + ([pl.BlockSpec(memory_space=pl.ANY)] if has_prev else []),
        out_specs=[pl.BlockSpec((chunk, w), lambda b, h, c: (rows(b, h, c), h)),
                   pl.BlockSpec((None, hb, LANES, LANES), lambda b, h, c: (b, h, 0, 0))],
        out_shape=[jax.ShapeDtypeStruct((p16.shape[0], heads * LANES), BF16),
                   jax.ShapeDtypeStruct((nb, heads, LANES, LANES), F32)],
        input_output_aliases={7: 0} if has_prev else {},
        scratch_shapes=[pltpu.VMEM((hb, LANES, LANES), F32)],
        compiler_params=_params("parallel", "parallel", "arbitrary"),
        name="hgrn",
    )(p16, p32, p16, p32, s0, lb_logits, hg_norm, *([prev] if has_prev else []))


def _cumsum_kernel(x_ref, o_ref, carry_ref):
    @pl.when(pl.program_id(0) == 0)
    def _():
        carry_ref[...] = jnp.zeros_like(carry_ref)

    x = x_ref[...]
    bw = x.shape[1]
    row = lax.broadcasted_iota(jnp.int32, (bw, bw), 0)
    col = lax.broadcasted_iota(jnp.int32, (bw, bw), 1)
    upper = (row <= col).astype(BF16)
    x1, x2, x3 = _split3(x)
    c = _dot(x1, upper) + _dot(x2, upper) + _dot(x3, upper) + carry_ref[...]
    o_ref[...] = c
    carry_ref[...] = c[:, bw - 1:bw]


def _cumsum_lanes(x):
    r, n = x.shape
    bw = 256 if n % 256 == 0 else LANES
    return pl.pallas_call(
        _cumsum_kernel,
        grid=(n // bw,),
        in_specs=[pl.BlockSpec((r, bw), lambda j: (0, j))],
        out_specs=pl.BlockSpec((r, bw), lambda j: (0, j)),
        out_shape=jax.ShapeDtypeStruct((r, n), F32),
        scratch_shapes=[pltpu.VMEM((r, 1), F32)],
        compiler_params=_params("arbitrary"),
        name="cumsum",
    )(x)


def _fox_update(s, v, m_ref, l_ref, acc_ref, slot, masked):
    tq, tk = s.shape
    if masked:
        row = lax.broadcasted_iota(jnp.int32, (tq, tk), 0)
        col = lax.broadcasted_iota(jnp.int32, (tq, tk), 1)
        s = jnp.where(row >= col, s, MASK_VALUE)
    m_old = m_ref[slot]
    m_new = jnp.maximum(m_old, jnp.max(s, axis=-1, keepdims=True))
    alpha = jnp.exp2(m_old - m_new)
    if tk % LANES == 0:
        ps = [jnp.exp2(s[:, c * LANES:(c + 1) * LANES] - m_new) for c in range(tk // LANES)]
        p_lanes = functools.reduce(jnp.add, ps)
        p = jnp.concatenate(ps, axis=1)
    else:
        p = jnp.exp2(s - m_new[:, :tk])
        p_lanes = jnp.concatenate([p, jnp.zeros((tq, LANES - tk), F32)], axis=1)
    l_ref[slot] = alpha * l_ref[slot] + p_lanes
    acc_ref[slot] = alpha * acc_ref[slot] + _dot(p.astype(BF16), v)
    m_ref[slot] = m_new


def _fox_init(m_ref, l_ref, acc_ref):
    m_ref[...] = jnp.full_like(m_ref, MASK_VALUE)
    l_ref[...] = jnp.zeros_like(l_ref)
    acc_ref[...] = jnp.zeros_like(acc_ref)


def _fox_out(l_ref, acc_ref, slot):
    return acc_ref[slot] / jnp.sum(l_ref[slot], axis=-1, keepdims=True)


def _fox_tile(q, k, v, bias, m_ref, l_ref, acc_ref, slot, masked):
    _fox_update(_dot_nt(q, k) + bias, v, m_ref, l_ref, acc_ref, slot, masked)


def _fox_prompt_kernel(q_ref, k_ref, v_ref, c_ref, o_ref, s_ref, m_ref, l_ref, acc_ref, kn_ref, *, tb, hp):
    qi = pl.program_id(1)
    nk = k_ref.shape[0] // tb
    chains = [(hh, half) for hh in range(hp) for half in range(2)]

    def lanes(hh):
        return slice(hh * LANES, (hh + 1) * LANES)

    @pl.when(qi == 0)
    def _():
        def kn_body(i, mx):
            kb = k_ref[pl.ds(pl.multiple_of(i * tb, tb), tb), :].astype(F32)
            for hh in range(hp):
                kh = kb[:, lanes(hh)]
                mx = jnp.maximum(mx, jnp.max(jnp.sum(kh * kh, axis=-1, keepdims=True), axis=0, keepdims=True))
            return mx
        kn_ref[...] = jnp.broadcast_to(lax.fori_loop(0, nk, kn_body, jnp.zeros((1, 1), F32)), kn_ref.shape)

    _fox_init(m_ref, l_ref, acc_ref)
    base = pl.multiple_of(qi * (2 * tb), 2 * tb)
    c_first = [c_ref[hh, :, pl.ds(base, LANES)][:, 0:1] for hh in range(hp)]
    qf = q_ref[...].astype(F32)
    qn2 = jnp.zeros((1, 1), F32)
    for hh in range(hp):
        qh = qf[:, lanes(hh)]
        qn2 = jnp.maximum(qn2, jnp.max(jnp.sum(qh * qh, axis=-1, keepdims=True), axis=0, keepdims=True))
    qk_max = jnp.sqrt(qn2 * kn_ref[:, 0:1])

    def scores(ch, slot, kstart):
        hh, half = chains[ch]
        bias = (c_first[hh] - c_ref[hh, :, pl.ds(kstart, tb)]) * LOG2E
        s_ref[ch, slot] = _dot_nt(q_ref[half * tb:(half + 1) * tb, lanes(hh)],
                                  k_ref[pl.ds(kstart, tb), lanes(hh)]) + bias

    def update(ch, slot, kstart, masked):
        hh, _ = chains[ch]
        _fox_update(s_ref[ch, slot], v_ref[pl.ds(kstart, tb), lanes(hh)], m_ref, l_ref, acc_ref, ch, masked)

    def tile_start(i):
        return pl.multiple_of(jnp.maximum(i, 0) * tb, tb)

    def live(kstart):
        bias_max = None
        for hh in range(hp):
            c_last = c_ref[hh, :, pl.ds(kstart + (tb - LANES), LANES)][:, LANES - 1:LANES]
            b = (c_first[hh] - c_last) * LOG2E
            bias_max = b if bias_max is None else jnp.maximum(bias_max, b)
        s_max = qk_max * (1.0 + 2.0 ** -6) + bias_max
        m_min = jnp.min(jnp.min(m_ref[...], axis=0), axis=0, keepdims=True)[:, 0:1]
        return jnp.where(s_max > m_min - EXP2_ZERO, 1, 0)[0, 0]

    top = pl.multiple_of(base + tb, tb)
    for ch, (hh, half) in enumerate(chains):
        if half == 1:
            scores(ch, 1, top)
        scores(ch, 0, base)
    for ch, (hh, half) in enumerate(chains):
        if half == 1:
            update(ch, 1, top, True)
        update(ch, 0, base, half == 0)

    n = 2 * qi
    for ch in range(len(chains)):
        scores(ch, 0, tile_start(n - 1))

    def body(carry):
        j, _ = carry
        k0, k1, k2 = tile_start(n - 1 - 2 * j), tile_start(n - 2 - 2 * j), tile_start(n - 3 - 2 * j)
        for ch in range(len(chains)):
            scores(ch, 1, k1)
            update(ch, 0, k0, False)
        for ch in range(len(chains)):
            scores(ch, 0, k2)
            update(ch, 1, k1, False)
        return j + 1, live(k2)

    lax.while_loop(lambda carry: jnp.logical_and(carry[0] < qi, carry[1] > 0), body,
                   (jnp.int32(0), live(tile_start(n - 1))))
    for ch, (hh, half) in enumerate(chains):
        o_ref[half * tb:(half + 1) * tb, lanes(hh)] = _fox_out(l_ref, acc_ref, ch).astype(o_ref.dtype)


def _fox_prompt(p16, kv16, c, *, heads, length, col_q, col_k, col_v, tb, hp):
    w = hp * LANES
    stat = pltpu.VMEM((2 * hp, tb, LANES), F32)
    return pl.pallas_call(
        functools.partial(_fox_prompt_kernel, tb=tb, hp=hp),
        grid=(heads // hp, length // (2 * tb)),
        in_specs=[pl.BlockSpec((2 * tb, w), lambda h, qi: (qi, col_q // hp + h)),
                  pl.BlockSpec((length, w), lambda h, qi: (0, col_k // hp + h)),
                  pl.BlockSpec((length, w), lambda h, qi: (0, col_v // hp + h)),
                  pl.BlockSpec((hp, 1, length), lambda h, qi: (h, 0, 0))],
        out_specs=pl.BlockSpec((2 * tb, w), lambda h, qi: (qi, h)),
        out_shape=jax.ShapeDtypeStruct((p16.shape[0], heads * LANES), BF16),
        scratch_shapes=[pltpu.VMEM((2 * hp, 2, tb, tb), F32), stat, stat, stat,
                        pltpu.VMEM((1, LANES), F32)],
        compiler_params=_params("parallel", "arbitrary"),
        name="fox_prompt",
    )(p16, kv16, kv16, c)


def _fox_sample_kernel(q_ref, kn_ref, vn_ref, kc_ref, vc_ref, cq_ref, ckc_ref, ckn_ref, ckl_ref, _, o_ref,
                       m_ref, l_ref, acc_ref, q1_ref, *, heads, nck):
    c = pl.program_id(1)
    tkc = kc_ref.shape[0] // heads
    cq_all = cq_ref[...] * LOG2E

    def head_lanes(h):
        return slice(h * LANES, (h + 1) * LANES)

    @pl.when(c == 0)
    def _():
        _fox_init(m_ref, l_ref, acc_ref)
        s_new = [_dot_nt(q_ref[:, head_lanes(h)], kn_ref[:, head_lanes(h)])
                 + (cq_all[:, h:h + 1] - ckn_ref[h] * LOG2E) for h in range(heads)]
        for h in range(heads):
            _fox_update(s_new[h], vn_ref[:, head_lanes(h)], m_ref, l_ref, acc_ref, h, True)
            q = q_ref[:, head_lanes(h)].astype(F32)
            q1 = jnp.max(jnp.sum(jnp.abs(q), axis=-1, keepdims=True), axis=0, keepdims=True)
            q1_ref[h] = jnp.broadcast_to(q1, (1, LANES))

    k_max = jnp.max(jnp.max(jnp.abs(kc_ref[...]), axis=0, keepdims=True), axis=-1, keepdims=True)
    cq_max = jnp.max(cq_all, axis=0, keepdims=True)
    c_last = ckl_ref[...] * LOG2E
    live = jnp.zeros((1, 1), jnp.int32)
    for h in range(heads):
        s_max = q1_ref[h][:, 0:1] * k_max * (1.0 + 2.0 ** -6) + (cq_max[:, h:h + 1] - c_last[:, h:h + 1])
        m_min = jnp.min(m_ref[h], axis=0, keepdims=True)[:, 0:1]
        live = jnp.maximum(live, jnp.where(s_max > m_min - EXP2_ZERO, 1, 0))

    @pl.when(live[0, 0] > 0)
    def _():
        s_all = []
        for h in range(heads):
            k = kc_ref[pl.ds(h, tkc, stride=heads), :].astype(BF16)
            s_all.append(_dot_nt(q_ref[:, head_lanes(h)], k) + (cq_all[:, h:h + 1] - ckc_ref[h] * LOG2E))
        for h in range(heads):
            v = vc_ref[pl.ds(h, tkc, stride=heads), :].astype(BF16)
            _fox_update(s_all[h], v, m_ref, l_ref, acc_ref, h, False)

    @pl.when(c == nck - 1)
    def _():
        for h in range(heads):
            o_ref[:, head_lanes(h)] = _fox_out(l_ref, acc_ref, h).astype(o_ref.dtype)


def _fox_sample(p16, kv16, cache_k, cache_v, cq, ckc, ckn, prev, *, layer, row0, nb, ls, heads,
                col_q, col_k, col_v, tkc):
    past = cache_k.shape[2] // heads
    nck = past // tkc
    rb0 = row0 // ls
    w = heads * LANES
    stat = pltpu.VMEM((heads, ls, LANES), F32)
    ckl = ckc[:, :, 0, tkc - 1::tkc].transpose(0, 2, 1)[:, :, None, :]

    def chunk(c):
        return nck - 1 - c

    return pl.pallas_call(
        functools.partial(_fox_sample_kernel, heads=heads, nck=nck),
        grid=(nb, nck),
        in_specs=[pl.BlockSpec((ls, w), lambda b, c: (rb0 + b, col_q // heads)),
                  pl.BlockSpec((ls, w), lambda b, c: (rb0 + b, col_k // heads)),
                  pl.BlockSpec((ls, w), lambda b, c: (rb0 + b, col_v // heads)),
                  pl.BlockSpec((None, None, tkc * heads, LANES), lambda b, c: (layer, b, chunk(c), 0)),
                  pl.BlockSpec((None, None, tkc * heads, LANES), lambda b, c: (layer, b, chunk(c), 0)),
                  pl.BlockSpec((None, ls, heads), lambda b, c: (b, 0, 0)),
                  pl.BlockSpec((None, heads, 1, tkc), lambda b, c: (b, 0, 0, chunk(c))),
                  pl.BlockSpec((None, heads, 1, ls), lambda b, c: (b, 0, 0, 0)),
                  pl.BlockSpec((None, None, 1, heads), lambda b, c: (b, chunk(c), 0, 0)),
                  pl.BlockSpec(memory_space=pl.ANY)],
        out_specs=pl.BlockSpec((ls, w), lambda b, c: (rb0 + b, 0)),
        out_shape=jax.ShapeDtypeStruct(prev.shape, BF16),
        input_output_aliases={9: 0},
        scratch_shapes=[stat, stat, stat, pltpu.VMEM((heads, 1, LANES), F32)],
        compiler_params=_params("parallel", "arbitrary"),
        name="fox_sample",
    )(p16, kv16, kv16, cache_k, cache_v, cq, ckc, ckn, ckl, prev)


def _row_tile(np_rows, ns_rows, ls):
    for tm in (512, 256, 128, 64, 32):
        if np_rows % tm == 0 and ns_rows % tm == 0 and tm % ls == 0:
            return tm
    raise ValueError("no row tile fits the prompt / sample row counts")


def kernel(x_prompt, x_sample, c_prompt, c_sample, cache_fox_k, cache_fox_v, cache_fox_logf,
           state_hgrn, ada_w, ada_b, ln_ffn1, ln_mix, ln_ffn2, ffn1_w_gate, ffn1_w_up,
           ffn1_w_down, ffn2_w_gate, ffn2_w_up, ffn2_w_down, w_in, hgrn_lb_logits,
           hgrn_norm_g, fox_f_bias, w_branch_a, w_branch_b, w_out, ln_final):
    bp, lp, d = x_prompt.shape
    bs, ls, _ = x_sample.shape
    depth = ada_w.shape[0]
    _, _, past, fh, fdh = cache_fox_k.shape
    _, _, hh, dk, dv = state_hgrn.shape
    assert bp == 1 and dk == LANES and dv == LANES and fdh == LANES
    assert ls % SUBLANES == 0 and d % LANES == 0
    np_rows, ns_rows = bp * lp, bs * ls
    t = np_rows + ns_rows
    hw, fw = hh * LANES, fh * LANES
    tm = _row_tile(np_rows, ns_rows, ls)
    npt, gt = np_rows // tm, tm // ls
    tn = 512 if (d % 512 == 0 and hw % 512 == 0 and fw % 512 == 0) else LANES
    chunk_p = 64 if lp % 64 == 0 else lp
    tb = 256 if lp % 512 == 0 else lp // 2
    tkc = 512 if past % 512 == 0 else past
    hb = next(n for n in (8, 4, 2, 1) if hh % n == 0)
    tm_kv = min(tm, 256)
    tm_big = 2 * tm if (np_rows % (2 * tm) == 0 and ns_rows % (2 * tm) == 0) else tm
    tiles = dict(tm=tm, npt=npt, gt=gt)
    tiles_big = dict(tm=tm_big, npt=np_rows // tm_big, gt=tm_big // ls)
    q_scale = LOG2E * float(fdh) ** -0.5

    x = jnp.concatenate([x_prompt.reshape(np_rows, d), x_sample.reshape(ns_rows, d)], axis=0)
    c_all = jnp.concatenate([c_prompt, c_sample], axis=0)
    m_pad = -(-c_all.shape[0] // SUBLANES) * SUBLANES
    c_pad = jnp.pad(c_all, ((0, m_pad - c_all.shape[0]), (0, 0)))
    s0_prompt = jnp.zeros((bp, hh, dk, dv), F32)

    sizes = (hw, hw, hw, hw, fw, fw, fw, fh, d, d)
    offs = [0]
    for s in sizes:
        offs.append(offs[-1] + s)
    o_aq, o_af, o_ai, o_ag, o_fq, o_fk, o_fv, o_ffl, o_ga, o_gb = offs[:10]

    outs = {k: [] for k in ("lfp", "sp", "lfs", "ss")}
    cache_k = cache_fox_k.reshape(depth, bs, past * fh, fdh)
    cache_v = cache_fox_v.reshape(depth, bs, past * fh, fdh)
    kv_out = None
    f1g, f1u, f1d = ffn1_w_gate.astype(BF16), ffn1_w_up.astype(BF16), ffn1_w_down.astype(BF16)
    f2g, f2u, f2d = ffn2_w_gate.astype(BF16), ffn2_w_up.astype(BF16), ffn2_w_down.astype(BF16)
    wba, wbb, wo = w_branch_a.astype(BF16), w_branch_b.astype(BF16), w_out.astype(BF16)
    wib = w_in.astype(BF16)
    wgate = wib[:, :, o_ga:o_ga + 2 * d]
    h = x
    for l in range(depth):
        wffl = jnp.pad(wib[l, :, o_ffl:o_ffl + fh], ((0, 0), (0, LANES - fh)))
        bffl = jnp.pad(fox_f_bias[l], (0, LANES - fh)).reshape(1, LANES)

        mod = _ada(c_pad, ada_w, ada_b, l)
        mod_p = mod[0:bp].reshape(bp, N_MOD, 1, d)
        mod_s = mod[bp:bp + bs].reshape(bs, N_MOD, 1, d)

        h = _ffn(h, ln_ffn1[l], f1g, f1u, f1d, l, mod_p, mod_s, 0, 1, 2, **tiles)

        n2 = _normmod(h, ln_mix[l], mod_p, mod_s, 3, 4, **tiles)
        p16 = _mm(n2, wib, BF16, [(o_aq, hw), (o_ai, hw), (o_fq, fw)], layer=l, tm=tm_big, tn=tn,
                  scale_group=2, scale=q_scale)
        p32 = _mm(n2, wib, F32, [(o_af, hw), (o_ag, hw)], layer=l, tm=tm_big, tn=tn)
        pg = _mm(n2, wgate, F32, [(0, 2 * d)], layer=l, tm=tm_big, tn=tn)
        kv16, *kv_out = _kvproj(n2, wib, kv_out, layer=l, col_k=o_fk, col_v=o_fv, depth=depth,
                                heads=fh, np_rows=np_rows, tm=tm_kv)
        lf_pad = _mm_logsig(n2, wffl, bffl, tm=tm)

        cb = lambda off: off // LANES
        hg_cols = dict(col_q=0, col_v=cb(hw), col_z=0, col_ag=cb(hw), hb=hb)
        o_all, s_p = _hgrn(p16, p32, s0_prompt, hgrn_lb_logits, hgrn_norm_g[l].reshape(1, hw), None,
                           layer=l, row0=0, nb=bp, length=lp, chunk=chunk_p, **hg_cols)
        o_all, s_s = _hgrn(p16, p32, state_hgrn[l], hgrn_lb_logits, hgrn_norm_g[l].reshape(1, hw), o_all,
                           layer=l, row0=np_rows, nb=bs, length=ls, chunk=ls, **hg_cols)

        lf = lf_pad[:, :fh]
        lf_p, lf_s = lf[:np_rows], lf[np_rows:]
        c_p = _cumsum_lanes(lf_p.T)
        cache_t = cache_fox_logf[l].transpose(0, 2, 1).reshape(bs * fh, past)
        new_t = lf_s.reshape(bs, ls, fh).transpose(0, 2, 1).reshape(bs * fh, ls)
        n_all = -(-(past + ls) // LANES) * LANES
        c_s = _cumsum_lanes(jnp.concatenate(
            [cache_t, new_t, jnp.zeros((bs * fh, n_all - past - ls), F32)], axis=1))
        cq_s = c_s[:, past:past + ls].reshape(bs, fh, ls).transpose(0, 2, 1)
        ckc_s = c_s[:, :past].reshape(bs, fh, 1, past)
        ckn_s = c_s[:, past:past + ls].reshape(bs, fh, 1, ls)

        fox_cols = dict(col_q=cb(2 * hw), col_k=0, col_v=cb(fw))
        y_all = _fox_prompt(p16, kv16, c_p.reshape(fh, 1, lp), heads=fh, length=lp, tb=tb,
                            hp=2 if fh % 2 == 0 else 1, **fox_cols)
        y_all = _fox_sample(p16, kv16, cache_k, cache_v, cq_s, ckc_s, ckn_s, y_all, layer=l,
                            row0=np_rows, nb=bs, ls=ls, heads=fh, tkc=tkc, **fox_cols)

        merged = _merge(o_all, y_all, wba, wbb, l, pg, 0, d, tm=tm_big, tn=tn)
        h = _outproj(merged, wo, l, h, mod_p, mod_s, 5, tn=tn, **tiles_big)

        h = _ffn(h, ln_ffn2[l], f2g, f2u, f2d, l, mod_p, mod_s, 6, 7, 8, **tiles)

        outs["lfp"].append(lf_p.reshape(bp, lp, fh))
        outs["sp"].append(s_p)
        outs["lfs"].append(lf_s.reshape(bs, ls, fh))
        outs["ss"].append(s_s)

    y_prompt = _final_norm(h, ln_final, row0=0, rows=np_rows, tm=tm).reshape(bp, lp, d)
    y_sample = _final_norm(h, ln_final, row0=np_rows, rows=ns_rows, tm=tm).reshape(bs, ls, d)
    st = {k: jnp.stack(v) for k, v in outs.items()}
    kp, vp, ks, vs = kv_out
    return (y_prompt, y_sample, kp.reshape(depth, bp, lp, fh, fdh), vp.reshape(depth, bp, lp, fh, fdh),
            st["lfp"], st["sp"], ks.reshape(depth, bs, ls, fh, fdh), vs.reshape(depth, bs, ls, fh, fdh),
            st["lfs"], st["ss"])
```
